```python
import math
import jax, jax.numpy as jnp
from jax import lax
import numpy as np


D_MODEL = 1024
BATCH = 2
SEQ = 8192
DEPTH = 2

N_MEM = 256
EPS = 1e-6
ATTN_HEADS = 8
ATTN_KV_HEADS = 2
ATTN_REP = ATTN_HEADS // ATTN_KV_HEADS
ATTN_HEAD_DIM = 64
ATTN_WIDTH = ATTN_HEADS * ATTN_HEAD_DIM
ATTN_KV_WIDTH = ATTN_KV_HEADS * ATTN_HEAD_DIM
WINDOW = 128
ATTN_BLOCK = 128
S5_WIDTH = 512
S5_GROUP = 16
S5_GROUPS = S5_WIDTH // S5_GROUP
S5_STATE = 64
S5_DT_MIN = 0.001
S5_DT_MAX = 0.1
SSD_INNER = D_MODEL
SSD_HEAD_DIM = 64
SSD_HEADS = SSD_INNER // SSD_HEAD_DIM
SSD_GROUPS = 2
SSD_STATE = 64
SSD_CONV = 4
SSD_CHUNK = 128
SSD_CONV_DIM = SSD_INNER + 2 * SSD_GROUPS * SSD_STATE
N_BRANCH = 3
IN_COLS = ATTN_WIDTH + 2 * ATTN_KV_WIDTH + S5_WIDTH + SSD_INNER + SSD_CONV_DIM + SSD_HEADS + N_BRANCH * D_MODEL
XA_HEADS = 4
XA_HEAD_DIM = 64
XA_WIDTH = XA_HEADS * XA_HEAD_DIM
PEER_HEADS = 8
PEER_KEY_DIM = 256
PEER_N_KEYS = 128
PEER_N_EXPERTS = PEER_N_KEYS * PEER_N_KEYS
PEER_TOPK = 16
PEER_TOKEN_BLOCK = 128

kernel_name = 'hybrid_gated_swa_s5_ssd_peer_block'


def rmsnorm(x, g):
    xf = x.astype(jnp.float32)
    y = xf * lax.rsqrt(jnp.mean(xf * xf, axis=-1, keepdims=True) + EPS)
    return (y * g.astype(jnp.float32)).astype(x.dtype)


def alibi_slopes(n):
    return 2.0 ** (-8.0 * (jnp.arange(n, dtype=jnp.float32) + 1.0) / n)


def _split_in_proj(proj):
    sizes = (ATTN_WIDTH, ATTN_KV_WIDTH, ATTN_KV_WIDTH, S5_WIDTH, SSD_INNER, SSD_CONV_DIM, SSD_HEADS)
    offs = []
    acc = 0
    for s in sizes:
        acc += s
        offs.append(acc)
    return jnp.split(proj, offs, axis=-1)


def sliding_window_attention(q, k, v, sinks):
    b, l, _ = q.shape
    nb = l // ATTN_BLOCK
    qb = q.reshape(b, nb, ATTN_BLOCK, ATTN_KV_HEADS, ATTN_REP, ATTN_HEAD_DIM)
    kb = k.reshape(b, nb, ATTN_BLOCK, ATTN_KV_HEADS, ATTN_HEAD_DIM)
    vb = v.reshape(b, nb, ATTN_BLOCK, ATTN_KV_HEADS, ATTN_HEAD_DIM)

    def with_prev(t):
        prev = jnp.concatenate([jnp.zeros_like(t[:, :1]), t[:, :-1]], axis=1)
        return jnp.concatenate([prev, t], axis=2)

    kk = with_prev(kb)
    vv = with_prev(vb)
    s = jnp.einsum('bnqgrd,bnkgd->bgrnqk', qb, kk).astype(jnp.float32) * (ATTN_HEAD_DIM ** -0.5)
    qi = jnp.arange(ATTN_BLOCK)[:, None]
    kj = jnp.arange(2 * ATTN_BLOCK)[None, :]
    dist = qi - kj + ATTN_BLOCK
    blk = jnp.arange(nb)[:, None, None]
    valid = (dist >= 0) & (dist < WINDOW) & ((blk > 0) | (kj >= ATTN_BLOCK))
    slopes = alibi_slopes(ATTN_HEADS).reshape(ATTN_KV_HEADS, ATTN_REP)
    s = s - slopes[:, :, None, None, None] * dist.astype(jnp.float32)
    s = jnp.where(valid, s, -jnp.inf)
    sink = jnp.broadcast_to(sinks.astype(jnp.float32).reshape(1, ATTN_KV_HEADS, ATTN_REP, 1, 1, 1), s.shape[:-1] + (1,))
    p = jax.nn.softmax(jnp.concatenate([s, sink], axis=-1), axis=-1)[..., :-1]
    o = jnp.einsum('bgrnqk,bnkgd->bnqgrd', p.astype(q.dtype), vv)
    return o.reshape(b, l, ATTN_WIDTH)


def _complex_linear_combine(e1, e2):
    a1r, a1i, b1r, b1i = e1
    a2r, a2i, b2r, b2i = e2
    return (a2r * a1r - a2i * a1i,
            a2r * a1i + a2i * a1r,
            a2r * b1r - a2i * b1i + b2r,
            a2r * b1i + a2i * b1r + b2i)


def s5_mixer(u, lam_re, lam_im, log_dt, b_re, b_im, c_re, c_im, d):
    bsz, l, _ = u.shape
    f32 = jnp.float32
    uf = u.astype(f32).reshape(bsz, l, S5_GROUPS, S5_GROUP)
    dt = jnp.exp(log_dt.astype(f32))[:, None]
    lr = lam_re.astype(f32)
    li = lam_im.astype(f32)
    mag = jnp.exp(lr * dt)
    ar = mag * jnp.cos(li * dt)
    ai = mag * jnp.sin(li * dt)
    den = lr * lr + li * li
    nr = ar - 1.0
    wr = (nr * lr + ai * li) / den
    wi = (ai * lr - nr * li) / den
    br_, bi_ = b_re.astype(f32), b_im.astype(f32)
    bbr = wr[..., None] * br_ - wi[..., None] * bi_
    bbi = wr[..., None] * bi_ + wi[..., None] * br_
    bu_r = jnp.einsum('blgi,gpi->blgp', uf, bbr)
    bu_i = jnp.einsum('blgi,gpi->blgp', uf, bbi)
    a_r = jnp.broadcast_to(ar, bu_r.shape)
    a_i = jnp.broadcast_to(ai, bu_i.shape)
    _, _, xr, xi = lax.associative_scan(_complex_linear_combine, (a_r, a_i, bu_r, bu_i), axis=1)
    y = jnp.einsum('blgp,gip->blgi', xr, c_re.astype(f32)) - jnp.einsum('blgp,gip->blgi', xi, c_im.astype(f32))
    y = y.reshape(bsz, l, S5_WIDTH) + d.astype(f32) * u.astype(f32)
    return y.astype(u.dtype)


def segsum(a):
    t = a.shape[-1]
    cs = jnp.cumsum(a, axis=-1)
    diff = cs[..., :, None] - cs[..., None, :]
    mask = jnp.tril(jnp.ones((t, t), dtype=bool))
    return jnp.where(mask, diff, -jnp.inf)


def ssd_scan(x, dt, a, bm, cm):
    b, l, h, p = x.shape
    n = bm.shape[-1]
    c = l // SSD_CHUNK
    xr = (x * dt[..., None]).reshape(b, c, SSD_CHUNK, h, p)
    br = bm.reshape(b, c, SSD_CHUNK, h, n)
    cr = cm.reshape(b, c, SSD_CHUNK, h, n)
    ad = (dt * a).reshape(b, c, SSD_CHUNK, h).transpose(0, 3, 1, 2)
    a_cs = jnp.cumsum(ad, axis=-1)
    lmat = jnp.exp(segsum(ad))
    scores = jnp.einsum('bclhn,bcshn->bhcls', cr, br) * lmat
    y_diag = jnp.einsum('bhcls,bcshp->bclhp', scores, xr)
    decay_states = jnp.exp(a_cs[..., -1:] - a_cs)
    states = jnp.einsum('bclhn,bhcl,bclhp->bchpn', br, decay_states, xr)
    states = jnp.concatenate([jnp.zeros_like(states[:, :1]), states], axis=1)
    chunk_decay = jnp.exp(segsum(jnp.pad(a_cs[..., -1], ((0, 0), (0, 0), (1, 0)))))
    states = jnp.einsum('bhzc,bchpn->bzhpn', chunk_decay, states)[:, :-1]
    y_off = jnp.einsum('bclhn,bchpn,bhcl->bclhp', cr, states, jnp.exp(a_cs))
    return (y_diag + y_off).reshape(b, l, h, p)


def ssd_mixer(z, xbc, dt_raw, conv_w, conv_b, dt_bias, a_log, d, norm_w):
    b, l, _ = xbc.shape
    f32 = jnp.float32
    xbc = lax.conv_general_dilated(xbc, conv_w[:, None, :], window_strides=(1,), padding=[(SSD_CONV - 1, 0)],
                                   dimension_numbers=('NWC', 'WIO', 'NWC'), feature_group_count=SSD_CONV_DIM) + conv_b
    xbc = jax.nn.silu(xbc)
    xs, bm, cm = jnp.split(xbc, [SSD_INNER, SSD_INNER + SSD_GROUPS * SSD_STATE], axis=-1)
    xs = xs.reshape(b, l, SSD_HEADS, SSD_HEAD_DIM).astype(f32)
    rep = SSD_HEADS // SSD_GROUPS
    bm = jnp.repeat(bm.reshape(b, l, SSD_GROUPS, SSD_STATE), rep, axis=2).astype(f32)
    cm = jnp.repeat(cm.reshape(b, l, SSD_GROUPS, SSD_STATE), rep, axis=2).astype(f32)
    dt = jax.nn.softplus(dt_raw.astype(f32) + dt_bias.astype(f32))
    a = -jnp.exp(a_log.astype(f32))
    y = ssd_scan(xs, dt, a, bm, cm) + d.astype(f32)[:, None] * xs
    y = y.reshape(b, l, SSD_INNER).astype(z.dtype)
    return rmsnorm(y * jax.nn.silu(z), norm_w)


def memory_cross_attention(h, mem, wq, wkv, wo):
    b, l, _ = h.shape
    m = mem.shape[1]
    q = (h @ wq).reshape(b, l, XA_HEADS, XA_HEAD_DIM)
    k, v = jnp.split(mem @ wkv, 2, axis=-1)
    k = k.reshape(b, m, XA_HEADS, XA_HEAD_DIM)
    v = v.reshape(b, m, XA_HEADS, XA_HEAD_DIM)
    s = jnp.einsum('blhd,bmhd->bhlm', q, k).astype(jnp.float32) * (XA_HEAD_DIM ** -0.5)
    p = jax.nn.softmax(s, axis=-1).astype(h.dtype)
    o = jnp.einsum('bhlm,bmhd->blhd', p, v).reshape(b, l, XA_WIDTH)
    return o @ wo


def peer_ffn(h, wq, k1, k2, u, v):
    b, l, dm = h.shape
    tokens = h.reshape(-1, PEER_TOKEN_BLOCK, dm)
    half = PEER_KEY_DIM // 2

    def block(xb):
        t = xb.shape[0]
        q = (xb @ wq).reshape(t, PEER_HEADS, 2, half)
        s1 = jnp.einsum('thd,kd->thk', q[:, :, 0], k1).astype(jnp.float32)
        s2 = jnp.einsum('thd,kd->thk', q[:, :, 1], k2).astype(jnp.float32)
        v1, i1 = lax.top_k(s1, PEER_TOPK)
        v2, i2 = lax.top_k(s2, PEER_TOPK)
        cand = (v1[..., :, None] + v2[..., None, :]).reshape(t, PEER_HEADS, PEER_TOPK * PEER_TOPK)
        cidx = (i1[..., :, None] * PEER_N_KEYS + i2[..., None, :]).reshape(t, PEER_HEADS, PEER_TOPK * PEER_TOPK)
        top, pos = lax.top_k(cand, PEER_TOPK)
        eidx = jnp.take_along_axis(cidx, pos, axis=-1)
        g = jax.nn.softmax(top, axis=-1)
        ue = jnp.take(u, eidx, axis=0)
        ve = jnp.take(v, eidx, axis=0)
        act = jax.nn.gelu(jnp.einsum('thkd,td->thk', ue, xb))
        return jnp.einsum('thk,thkd->td', (g * act.astype(jnp.float32)).astype(xb.dtype), ve)

    out = lax.map(block, tokens)
    return out.reshape(b, l, dm)


def setup_inputs(seed: int = 0) -> dict:
    key = jax.random.key(seed)
    ks = iter(jax.random.split(key, 40))
    f32 = jnp.float32
    L = DEPTH

    def nrm(shape, scale):
        return scale * jax.random.normal(next(ks), shape, f32)

    def gain(shape):
        return 1.0 + nrm(shape, 0.02)

    x = nrm((BATCH, SEQ, D_MODEL), 1.0)
    mem = nrm((BATCH, N_MEM, D_MODEL), 1.0)
    norm_mix = gain((L, D_MODEL))
    w_in = nrm((L, D_MODEL, IN_COLS), D_MODEL ** -0.5)
    attn_sinks = nrm((L, ATTN_HEADS), 1.0)
    s5_lambda_re = -0.5 + nrm((L, S5_GROUPS, S5_STATE), 0.01)
    s5_lambda_im = math.pi * jnp.arange(S5_STATE, dtype=f32) + nrm((L, S5_GROUPS, S5_STATE), 0.01)
    s5_log_dt = jax.random.uniform(next(ks), (L, S5_GROUPS), f32, math.log(S5_DT_MIN), math.log(S5_DT_MAX))
    s5_b_re = nrm((L, S5_GROUPS, S5_STATE, S5_GROUP), (2 * S5_GROUP) ** -0.5)
    s5_b_im = nrm((L, S5_GROUPS, S5_STATE, S5_GROUP), (2 * S5_GROUP) ** -0.5)
    s5_c_re = nrm((L, S5_GROUPS, S5_GROUP, S5_STATE), (2 * S5_STATE) ** -0.5)
    s5_c_im = nrm((L, S5_GROUPS, S5_GROUP, S5_STATE), (2 * S5_STATE) ** -0.5)
    s5_d = nrm((L, S5_WIDTH), 1.0)
    s5_glu_a = nrm((L, S5_WIDTH, D_MODEL), S5_WIDTH ** -0.5)
    s5_glu_b = nrm((L, S5_WIDTH, D_MODEL), S5_WIDTH ** -0.5)
    ssd_conv_w = nrm((L, SSD_CONV, SSD_CONV_DIM), SSD_CONV ** -0.5)
    ssd_conv_b = nrm((L, SSD_CONV_DIM), 0.02)
    dt0 = jnp.exp(jax.random.uniform(next(ks), (L, SSD_HEADS), f32, math.log(0.001), math.log(0.1)))
    ssd_dt_bias = dt0 + jnp.log(-jnp.expm1(-dt0))
    ssd_a_log = jnp.log(jax.random.uniform(next(ks), (L, SSD_HEADS), f32, 1.0, 16.0))
    ssd_d = gain((L, SSD_HEADS))
    ssd_norm = gain((L, SSD_INNER))
    w_attn_out = nrm((L, ATTN_WIDTH, D_MODEL), ATTN_WIDTH ** -0.5)
    w_ssd_out = nrm((L, SSD_INNER, D_MODEL), SSD_INNER ** -0.5)
    w_o = nrm((L, D_MODEL, D_MODEL), D_MODEL ** -0.5)
    norm_xattn = gain((L, D_MODEL))
    norm_mem = gain((L, D_MODEL))
    xa_wq = nrm((L, D_MODEL, XA_WIDTH), D_MODEL ** -0.5)
    xa_wkv = nrm((L, D_MODEL, 2 * XA_WIDTH), D_MODEL ** -0.5)
    xa_wo = nrm((L, XA_WIDTH, D_MODEL), XA_WIDTH ** -0.5)
    norm_ffn = gain((L, D_MODEL))
    peer_wq = nrm((L, D_MODEL, PEER_HEADS * PEER_KEY_DIM), D_MODEL ** -0.5)
    peer_k1 = nrm((L, PEER_N_KEYS, PEER_KEY_DIM // 2), (PEER_KEY_DIM // 2) ** -0.5)
    peer_k2 = nrm((L, PEER_N_KEYS, PEER_KEY_DIM // 2), (PEER_KEY_DIM // 2) ** -0.5)
    peer_u = nrm((L, PEER_N_EXPERTS, D_MODEL), D_MODEL ** -0.5)
    peer_v = nrm((L, PEER_N_EXPERTS, D_MODEL), (PEER_HEADS * PEER_TOPK) ** -0.5 * 4.0)
    norm_final = gain((D_MODEL,))
    return {'x': x, 'mem': mem, 'norm_mix': norm_mix, 'w_in': w_in, 'attn_sinks': attn_sinks,
            's5_lambda_re': s5_lambda_re, 's5_lambda_im': s5_lambda_im, 's5_log_dt': s5_log_dt,
            's5_b_re': s5_b_re, 's5_b_im': s5_b_im, 's5_c_re': s5_c_re, 's5_c_im': s5_c_im, 's5_d': s5_d,
            's5_glu_a': s5_glu_a, 's5_glu_b': s5_glu_b, 'ssd_conv_w': ssd_conv_w, 'ssd_conv_b': ssd_conv_b,
            'ssd_dt_bias': ssd_dt_bias, 'ssd_a_log': ssd_a_log, 'ssd_d': ssd_d, 'ssd_norm': ssd_norm,
            'w_attn_out': w_attn_out, 'w_ssd_out': w_ssd_out, 'w_o': w_o, 'norm_xattn': norm_xattn,
            'norm_mem': norm_mem, 'xa_wq': xa_wq, 'xa_wkv': xa_wkv, 'xa_wo': xa_wo, 'norm_ffn': norm_ffn,
            'peer_wq': peer_wq, 'peer_k1': peer_k1, 'peer_k2': peer_k2, 'peer_u': peer_u, 'peer_v': peer_v,
            'norm_final': norm_final}


def reference(x, mem, norm_mix, w_in, attn_sinks, s5_lambda_re, s5_lambda_im, s5_log_dt, s5_b_re, s5_b_im,
              s5_c_re, s5_c_im, s5_d, s5_glu_a, s5_glu_b, ssd_conv_w, ssd_conv_b, ssd_dt_bias, ssd_a_log, ssd_d,
              ssd_norm, w_attn_out, w_ssd_out, w_o, norm_xattn, norm_mem, xa_wq, xa_wkv, xa_wo, norm_ffn,
              peer_wq, peer_k1, peer_k2, peer_u, peer_v, norm_final):
    b, l, _ = x.shape
    for i in range(DEPTH):
        hn = rmsnorm(x, norm_mix[i])
        proj = hn @ w_in[i]
        q, k, v, s5_u, ssd_z, ssd_xbc, ssd_dt, gates = _split_in_proj(proj)
        branch_a = sliding_window_attention(q, k, v, attn_sinks[i]) @ w_attn_out[i]
        ys5 = jax.nn.gelu(s5_mixer(s5_u, s5_lambda_re[i], s5_lambda_im[i], s5_log_dt[i], s5_b_re[i], s5_b_im[i],
                                   s5_c_re[i], s5_c_im[i], s5_d[i]))
        branch_b = (ys5 @ s5_glu_a[i]) * jax.nn.sigmoid(ys5 @ s5_glu_b[i])
        branch_c = ssd_mixer(ssd_z, ssd_xbc, ssd_dt, ssd_conv_w[i], ssd_conv_b[i], ssd_dt_bias[i], ssd_a_log[i],
                             ssd_d[i], ssd_norm[i]) @ w_ssd_out[i]
        g = jax.nn.sigmoid(gates.reshape(b, l, N_BRANCH, D_MODEL))
        merged = g[:, :, 0] * branch_a + g[:, :, 1] * branch_b + g[:, :, 2] * branch_c
        x = x + merged @ w_o[i]
        x = x + memory_cross_attention(rmsnorm(x, norm_xattn[i]), rmsnorm(mem, norm_mem[i]), xa_wq[i], xa_wkv[i], xa_wo[i])
        x = x + peer_ffn(rmsnorm(x, norm_ffn[i]), peer_wq[i], peer_k1[i], peer_k2[i], peer_u[i], peer_v[i])
    return rmsnorm(x, norm_final)
```

```python
import functools
import math

import jax
import jax.numpy as jnp
import numpy as np
from jax import lax
from jax.experimental import pallas as pl
from jax.experimental.pallas import tpu as pltpu

F32 = jnp.float32
BF16 = jnp.bfloat16
NEG_INF = float("-inf")

D_MODEL = 1024
EPS = 1e-6
ATTN_HEADS = 8
ATTN_HEAD_DIM = 64
ATTN_WIDTH = ATTN_HEADS * ATTN_HEAD_DIM
ATTN_BLOCK = 128
S5_WIDTH = 512
S5_GROUP = 16
S5_GROUPS = S5_WIDTH // S5_GROUP
S5_STATE = 64
S5_LANES = S5_GROUPS * S5_STATE
S5_CHUNK = 256
S5_SCAN_WIDTH = 512
SSD_INNER = 1024
SSD_HEAD_DIM = 64
SSD_HEADS = SSD_INNER // SSD_HEAD_DIM
SSD_STATE = 64
SSD_GROUPS = 2
SSD_CONV = 4
SSD_CHUNK = 128
SSD_BC = SSD_GROUPS * SSD_STATE
SSD_CONV_DIM = SSD_INNER + 2 * SSD_BC
XA_HEADS = 4
XA_HEAD_DIM = 64
XA_WIDTH = XA_HEADS * XA_HEAD_DIM
PEER_HEADS = 8
PEER_KEY_DIM = 256
PEER_HALF = PEER_KEY_DIM // 2
PEER_N_KEYS = 128
PEER_TOPK = 16
PEER_N_EXPERTS = PEER_N_KEYS * PEER_N_KEYS
LANES = 128
SUBLANES = 8
VMEM_LIMIT_BYTES = 56 * 1024 * 1024

BIG_COLS = 6144
SMALL_COLS = 640


def _params(*sem):
    return pltpu.CompilerParams(dimension_semantics=sem, vmem_limit_bytes=VMEM_LIMIT_BYTES)


def _dot(a, b):
    return jnp.dot(a, b, preferred_element_type=F32)


def _dot_nt(a, b):
    return lax.dot_general(a, b, (((1,), (1,)), ((), ())), preferred_element_type=F32)


def _split3(x):
    p1 = x.astype(BF16)
    r1 = x - p1.astype(F32)
    p2 = r1.astype(BF16)
    p3 = (r1 - p2.astype(F32)).astype(BF16)
    return p1, p2, p3


def _exact_dot_rhs(sel, x):
    p1, p2, p3 = _split3(x)
    return _dot(sel, p1) + _dot(sel, p2) + _dot(sel, p3)


def _exact_dot_lhs(x, sel):
    p1, p2, p3 = _split3(x)
    return _dot(p1, sel) + _dot(p2, sel) + _dot(p3, sel)


def _norm_mm_kernel(x_ref, g_ref, w_ref, o_ref, *rest, with_h):
    if with_h:
        hout_ref, h_ref = rest
    else:
        (h_ref,) = rest

    @pl.when(pl.program_id(1) == 0)
    def _():
        x = x_ref[...]
        h = x * lax.rsqrt(jnp.mean(x * x, axis=-1, keepdims=True) + EPS) * g_ref[...]
        h_ref[...] = h.astype(BF16)
        if with_h:
            hout_ref[...] = h.astype(BF16)

    o_ref[...] = _dot(h_ref[...], w_ref[...]).astype(o_ref.dtype)


def _norm_mm(x, g, w, *, tm, tn, out_dtype, with_h=False, name):
    n, k = x.shape
    m = w.shape[1]
    assert n % tm == 0 and m % tn == 0
    out_shape = [jax.ShapeDtypeStruct((n, m), out_dtype)]
    out_specs = [pl.BlockSpec((tm, tn), lambda i, j: (i, j))]
    if with_h:
        out_shape.append(jax.ShapeDtypeStruct((n, k), BF16))
        out_specs.append(pl.BlockSpec((tm, k), lambda i, j: (i, 0)))
    res = pl.pallas_call(
        functools.partial(_norm_mm_kernel, with_h=with_h),
        grid=(n // tm, m // tn),
        in_specs=[
            pl.BlockSpec((tm, k), lambda i, j: (i, 0)),
            pl.BlockSpec((1, k), lambda i, j: (0, 0)),
            pl.BlockSpec((k, tn), lambda i, j: (0, j)),
        ],
        out_specs=out_specs,
        out_shape=out_shape,
        scratch_shapes=[pltpu.VMEM((tm, k), BF16)],
        compiler_params=_params("parallel", "arbitrary"),
        name=name,
    )(x, g.reshape(1, k), w)
    return res if with_h else res[0]


def _swa_kernel(sink_ref, q_ref, kc_ref, kp_ref, vc_ref, vp_ref, o_ref):
    nblk = pl.program_id(1)
    blk = ATTN_BLOCK
    q = q_ref[...]
    k = jnp.concatenate([kp_ref[...], kc_ref[...]], axis=0)
    v = jnp.concatenate([vp_ref[...], vc_ref[...]], axis=0).astype(BF16)
    lane_k = lax.broadcasted_iota(jnp.int32, (2 * blk, LANES), 1)
    k_lo = jnp.where(lane_k < ATTN_HEAD_DIM, k, 0.0).astype(BF16)
    k_hi = jnp.where(lane_k < ATTN_HEAD_DIM, 0.0, k).astype(BF16)
    qi = lax.broadcasted_iota(jnp.int32, (blk, 2 * blk), 0)
    kj = lax.broadcasted_iota(jnp.int32, (blk, 2 * blk), 1)
    dist = qi - kj + blk
    valid = (dist >= 0) & (dist < blk) & ((nblk > 0) | (kj >= blk))
    distf = dist.astype(F32)
    lane_o = lax.broadcasted_iota(jnp.int32, (blk, LANES), 1)
    outs = []
    for j in range(ATTN_HEADS // 2):
        qs = (q[:, j * LANES:(j + 1) * LANES] * (ATTN_HEAD_DIM ** -0.5)).astype(BF16)
        res = []
        for half, kk in ((0, k_lo), (1, k_hi)):
            head = j + 4 * half
            slope = 2.0 ** (-8.0 * (head + 1) / ATTN_HEADS)
            s = _dot_nt(qs, kk) - slope * distf
            s = jnp.where(valid, s, NEG_INF)
            sink = sink_ref[head]
            m = jnp.maximum(jnp.max(s, axis=-1, keepdims=True), sink)
            p = jnp.exp(s - m)
            denom = jnp.sum(p, axis=-1, keepdims=True) + jnp.exp(sink - m)
            p = p / denom
            res.append(_dot(p.astype(BF16), v))
        outs.append(jnp.where(lane_o < ATTN_HEAD_DIM, res[0], res[1]))
    o_ref[...] = jnp.concatenate(outs, axis=1).astype(o_ref.dtype)


def _swa(big, small, sinks, *, batch, seq):
    nb = seq // ATTN_BLOCK
    n = batch * seq
    qcol = 5120 // ATTN_WIDTH

    def cur(col):
        return lambda b, i: (b * nb + i, col)

    def prev(col):
        return lambda b, i: (b * nb + jnp.maximum(i - 1, 0), col)

    return pl.pallas_call(
        _swa_kernel,
        grid=(batch, nb),
        in_specs=[
            pl.BlockSpec(memory_space=pltpu.SMEM),
            pl.BlockSpec((ATTN_BLOCK, ATTN_WIDTH), cur(qcol)),
            pl.BlockSpec((ATTN_BLOCK, LANES), cur(0)),
            pl.BlockSpec((ATTN_BLOCK, LANES), prev(0)),
            pl.BlockSpec((ATTN_BLOCK, LANES), cur(1)),
            pl.BlockSpec((ATTN_BLOCK, LANES), prev(1)),
        ],
        out_specs=pl.BlockSpec((ATTN_BLOCK, ATTN_WIDTH), lambda b, i: (b * nb + i, 0)),
        out_shape=jax.ShapeDtypeStruct((n, ATTN_WIDTH), BF16),
        compiler_params=_params("parallel", "arbitrary"),
        name="swa",
    )(sinks, big, small, small, small, small)


def _s5_kernel(u_ref, wb_ref, wc_ref, d_ref, sc_ref, o_ref, xr_ref, xi_ref, cr_ref, ci_ref):
    @pl.when(pl.program_id(1) == 0)
    def _():
        cr_ref[...] = jnp.zeros_like(cr_ref)
        ci_ref[...] = jnp.zeros_like(ci_ref)

    u = u_ref[...]
    bu = _dot(u.astype(BF16), wb_ref[...])
    xr_ref[...] = bu[:, :S5_LANES]
    xi_ref[...] = bu[:, S5_LANES:]

    def cmul_add(xr, xi, ar, ai, sr, si):
        return xr + (ar * sr - ai * si), xi + (ar * si + ai * sr)

    for w in range(S5_LANES // S5_SCAN_WIDTH):
        sl = slice(w * S5_SCAN_WIDTH, (w + 1) * S5_SCAN_WIDTH)
        consts = [sc_ref[c, :, sl] for c in range(8)]
        a1r, a1i, a2r, a2i, a4r, a4i, pr, pi = consts

        def body(t, carry):
            cr, ci = carry
            r0 = pl.multiple_of(t * SUBLANES, SUBLANES)
            xr = xr_ref[pl.ds(r0, SUBLANES), sl]
            xi = xi_ref[pl.ds(r0, SUBLANES), sl]
            xr, xi = cmul_add(xr, xi, a1r, a1i, pltpu.roll(xr, 1, 0), pltpu.roll(xi, 1, 0))
            xr, xi = cmul_add(xr, xi, a2r, a2i, pltpu.roll(xr, 2, 0), pltpu.roll(xi, 2, 0))
            xr, xi = cmul_add(xr, xi, a4r, a4i, pltpu.roll(xr, 4, 0), pltpu.roll(xi, 4, 0))
            xr, xi = cmul_add(xr, xi, pr, pi, cr, ci)
            xr_ref[pl.ds(r0, SUBLANES), sl] = xr
            xi_ref[pl.ds(r0, SUBLANES), sl] = xi
            return xr[SUBLANES - 1:SUBLANES, :], xi[SUBLANES - 1:SUBLANES, :]

        cr, ci = lax.fori_loop(0, S5_CHUNK // SUBLANES, body, (cr_ref[0:1, sl], ci_ref[0:1, sl]))
        cr_ref[0:1, sl] = cr
        ci_ref[0:1, sl] = ci

    xcat = jnp.concatenate([xr_ref[...].astype(BF16), xi_ref[...].astype(BF16)], axis=1)
    y = _dot(xcat, wc_ref[...]) + d_ref[...] * u
    o_ref[...] = jax.nn.gelu(y).astype(o_ref.dtype)


def _s5_prepare(lam_re, lam_im, log_dt, b_re, b_im, c_re, c_im):
    dt = jnp.exp(log_dt.astype(F32))[:, None]
    lr = lam_re.astype(F32)
    li = lam_im.astype(F32)
    mag = jnp.exp(lr * dt)
    ar = mag * jnp.cos(li * dt)
    ai = mag * jnp.sin(li * dt)
    den = lr * lr + li * li
    nr = ar - 1.0
    wr = (nr * lr + ai * li) / den
    wi = (ai * lr - nr * li) / den
    br_, bi_ = b_re.astype(F32), b_im.astype(F32)
    bbr = wr[..., None] * br_ - wi[..., None] * bi_
    bbi = wr[..., None] * bi_ + wi[..., None] * br_
    eye = jnp.eye(S5_GROUPS, dtype=F32)
    wbr = jnp.einsum('gpi,gh->gihp', bbr, eye).reshape(S5_WIDTH, S5_LANES)
    wbi = jnp.einsum('gpi,gh->gihp', bbi, eye).reshape(S5_WIDTH, S5_LANES)
    wb = jnp.concatenate([wbr, wbi], axis=1).astype(BF16)
    wcr = jnp.einsum('gip,gh->gphi', c_re.astype(F32), eye).reshape(S5_LANES, S5_WIDTH)
    wci = jnp.einsum('gip,gh->gphi', c_im.astype(F32), eye).reshape(S5_LANES, S5_WIDTH)
    wc = jnp.concatenate([wcr, -wci], axis=0).astype(BF16)
    ar_f = ar.reshape(1, S5_LANES)
    ai_f = ai.reshape(1, S5_LANES)

    def cmul(xr, xi, yr, yi):
        return xr * yr - xi * yi, xr * yi + xi * yr

    pows_r, pows_i = [ar_f], [ai_f]
    for _ in range(SUBLANES - 1):
        nr_, ni_ = cmul(pows_r[-1], pows_i[-1], ar_f, ai_f)
        pows_r.append(nr_)
        pows_i.append(ni_)
    t_idx = jnp.arange(SUBLANES)[:, None]

    def masked(k):
        return (jnp.where(t_idx >= k, pows_r[k - 1], 0.0), jnp.where(t_idx >= k, pows_i[k - 1], 0.0))

    a1 = masked(1)
    a2 = masked(2)
    a4 = masked(4)
    pr = jnp.concatenate(pows_r, axis=0)
    pi = jnp.concatenate(pows_i, axis=0)
    sc = jnp.stack([a1[0], a1[1], a2[0], a2[1], a4[0], a4[1], pr, pi], axis=0)
    return wb, wc, sc


def _s5(big, wb, wc, d, sc, *, batch, seq):
    n = batch * seq
    nc = seq // S5_CHUNK
    ucol = 5632 // S5_WIDTH
    return pl.pallas_call(
        _s5_kernel,
        grid=(batch, nc),
        in_specs=[
            pl.BlockSpec((S5_CHUNK, S5_WIDTH), lambda b, c: (b * nc + c, ucol)),
            pl.BlockSpec((S5_WIDTH, 2 * S5_LANES), lambda b, c: (0, 0)),
            pl.BlockSpec((2 * S5_LANES, S5_WIDTH), lambda b, c: (0, 0)),
            pl.BlockSpec((1, S5_WIDTH), lambda b, c: (0, 0)),
            pl.BlockSpec((8, SUBLANES, S5_LANES), lambda b, c: (0, 0, 0)),
        ],
        out_specs=pl.BlockSpec((S5_CHUNK, S5_WIDTH), lambda b, c: (b * nc + c, 0)),
        out_shape=jax.ShapeDtypeStruct((n, S5_WIDTH), BF16),
        scratch_shapes=[
            pltpu.VMEM((S5_CHUNK, S5_LANES), F32),
            pltpu.VMEM((S5_CHUNK, S5_LANES), F32),
            pltpu.VMEM((SUBLANES, S5_LANES), F32),
            pltpu.VMEM((SUBLANES, S5_LANES), F32),
        ],
        compiler_params=_params("parallel", "arbitrary"),
        name="s5",
    )(big, wb, wc, d.reshape(1, S5_WIDTH), sc)


def _ssd_kernel(z_ref, xs_ref, b_ref, c_ref, dt_ref, cw_ref, cb_ref, dtb_ref, alog_ref, dvec_ref, nw_ref,
                exp_h_ref, exp_l_ref, o_ref, buf_ref, st_ref):
    q = SSD_CHUNK

    @pl.when(pl.program_id(1) == 0)
    def _():
        buf_ref[0:SUBLANES, :] = jnp.zeros((SUBLANES, SSD_CONV_DIM), F32)
        st_ref[...] = jnp.zeros_like(st_ref)

    buf_ref[SUBLANES:SUBLANES + q, 0:SSD_INNER] = xs_ref[...]
    buf_ref[SUBLANES:SUBLANES + q, SSD_INNER:SSD_INNER + SSD_BC] = b_ref[...]
    buf_ref[SUBLANES:SUBLANES + q, SSD_INNER + SSD_BC:SSD_CONV_DIM] = c_ref[...]
    acc = jnp.broadcast_to(cb_ref[...], (q, SSD_CONV_DIM))
    for k in range(SSD_CONV):
        acc = acc + cw_ref[k:k + 1, :] * buf_ref[pl.ds(SUBLANES - (SSD_CONV - 1) + k, q), :]
    buf_ref[0:SUBLANES, :] = buf_ref[q:q + SUBLANES, :]
    xbc = acc * jax.nn.sigmoid(acc)
    xs = xbc[:, :SSD_INNER]
    bm = xbc[:, SSD_INNER:SSD_INNER + SSD_BC]
    cm = xbc[:, SSD_INNER + SSD_BC:]

    lane = lax.broadcasted_iota(jnp.int32, (q, LANES), 1)
    row = lax.broadcasted_iota(jnp.int32, (q, LANES), 0)
    head_lane = lane < SSD_HEADS
    dt_in = dt_ref[...] + dtb_ref[...]
    dt = jnp.maximum(dt_in, 0.0) + jnp.log1p(jnp.exp(-jnp.abs(dt_in)))
    dt = jnp.where(head_lane, dt, 0.0)
    a = -jnp.exp(alog_ref[...])
    ad = dt * a
    tri = (row >= lane).astype(BF16)
    cs = _exact_dot_rhs(tri, ad)
    cs_t = cs.T
    dt_t = dt.T
    total = cs[q - 1:q, :]
    dec_t = jnp.exp(cs_t[:, q - 1:q] - cs_t)
    w_t = dec_t * dt_t

    exp_h = exp_h_ref[...]
    exp_l = exp_l_ref[...]
    ecs_x = _exact_dot_lhs(jnp.exp(cs), exp_h)
    w_x = _exact_dot_lhs(w_t.T, exp_h)
    cs_cols = _exact_dot_lhs(cs, exp_l)
    tot_x = _exact_dot_lhs(jnp.broadcast_to(jnp.exp(total), (SUBLANES, LANES)), exp_h)[0:1, :]

    bm16 = bm.astype(BF16)
    cm16 = cm.astype(BF16)
    xs16 = xs.astype(BF16)
    low = lane < SSD_STATE
    g_mats = [_dot_nt(jnp.where(low, cm, 0.0).astype(BF16), bm16),
              _dot_nt(jnp.where(low, 0.0, cm).astype(BF16), bm16)]
    causal = row >= lane
    ydiag = []
    for j in range(SSD_HEADS // 2):
        slab = xs[:, j * LANES:(j + 1) * LANES]
        halves = (jnp.where(low, slab, 0.0).astype(BF16), jnp.where(low, 0.0, slab).astype(BF16))
        acc_j = None
        for half in range(2):
            h = 2 * j + half
            g = h // (SSD_HEADS // SSD_GROUPS)
            seg = cs_cols[:, h * LANES:(h + 1) * LANES] - cs_t[h:h + 1, :]
            lmat = jnp.exp(jnp.where(causal, seg, NEG_INF))
            m = (g_mats[g] * lmat * dt_t[h:h + 1, :]).astype(BF16)
            part = _dot(m, halves[half])
            acc_j = part if acc_j is None else acc_j + part
        ydiag.append(acc_j)
    y = jnp.concatenate(ydiag, axis=1)

    st = st_ref[...]
    y = y + _dot(cm16, st.astype(BF16)) * ecs_x
    new = _dot(bm.T.astype(BF16), (xs * w_x).astype(BF16))
    srow = lax.broadcasted_iota(jnp.int32, (LANES, SSD_INNER), 0)
    scol = lax.broadcasted_iota(jnp.int32, (LANES, SSD_INNER), 1)
    same_group = (srow < SSD_STATE) == (scol < SSD_INNER // SSD_GROUPS)
    st_ref[...] = jnp.where(same_group, st * tot_x + new, 0.0)

    y = y + dvec_ref[...] * xs
    zz = z_ref[...]
    gated = y * (zz * jax.nn.sigmoid(zz))
    out = gated * lax.rsqrt(jnp.mean(gated * gated, axis=-1, keepdims=True) + EPS) * nw_ref[...]
    o_ref[...] = out.astype(o_ref.dtype)


def _ssd(big, small, conv_w, conv_b, dt_bias, a_log, dvec, norm_w, *, batch, seq):
    n = batch * seq
    nc = seq // SSD_CHUNK
    q = SSD_CHUNK
    pad = LANES - SSD_HEADS
    dtb = jnp.pad(dt_bias.astype(F32), (0, pad)).reshape(1, LANES)
    alog = jnp.pad(a_log.astype(F32), (0, pad)).reshape(1, LANES)
    d_x = jnp.repeat(dvec.astype(F32), SSD_HEAD_DIM).reshape(1, SSD_INNER)
    heads = np.arange(LANES)[:, None]
    exp_h = jnp.asarray(heads == (np.arange(SSD_INNER)[None, :] // SSD_HEAD_DIM), dtype=BF16)
    exp_l = jnp.asarray(heads == (np.arange(SSD_HEADS * LANES)[None, :] // LANES), dtype=BF16)

    def blk(col, width):
        return pl.BlockSpec((q, width), lambda b, c: (b * nc + c, col))

    def full(shape):
        return pl.BlockSpec(shape, lambda b, c: (0,) * len(shape))

    return pl.pallas_call(
        _ssd_kernel,
        grid=(batch, nc),
        in_specs=[
            blk(0, SSD_INNER), blk(1, SSD_INNER),
            blk(2, LANES), blk(3, LANES), blk(4, LANES),
            full((SSD_CONV, SSD_CONV_DIM)), full((1, SSD_CONV_DIM)),
            full((1, LANES)), full((1, LANES)), full((1, SSD_INNER)), full((1, SSD_INNER)),
            full((LANES, SSD_INNER)), full((LANES, SSD_HEADS * LANES)),
        ],
        out_specs=pl.BlockSpec((q, SSD_INNER), lambda b, c: (b * nc + c, 0)),
        out_shape=jax.ShapeDtypeStruct((n, SSD_INNER), BF16),
        scratch_shapes=[
            pltpu.VMEM((q + SUBLANES, SSD_CONV_DIM), F32),
            pltpu.VMEM((LANES, SSD_INNER), F32),
        ],
        compiler_params=_params("parallel", "arbitrary"),
        name="ssd",
    )(big, big, small, small, small, conv_w.astype(F32), conv_b.astype(F32).reshape(1, SSD_CONV_DIM),
      dtb, alog, d_x, norm_w.astype(F32).reshape(1, SSD_INNER), exp_h, exp_l)


def _merge_kernel(attn_ref, ys5_ref, ssd_ref, ga_ref, gb_ref, gc_ref, x_ref,
                  wa_ref, wga_ref, wgb_ref, wc_ref, wo_ref, o_ref):
    ys5 = ys5_ref[...]
    br_a = _dot(attn_ref[...], wa_ref[...])
    br_b = _dot(ys5, wga_ref[...]) * jax.nn.sigmoid(_dot(ys5, wgb_ref[...]))
    br_c = _dot(ssd_ref[...], wc_ref[...])
    merged = (jax.nn.sigmoid(ga_ref[...]) * br_a + jax.nn.sigmoid(gb_ref[...]) * br_b
              + jax.nn.sigmoid(gc_ref[...]) * br_c)
    o_ref[...] = x_ref[...] + _dot(merged.astype(BF16), wo_ref[...])


def _merge(attn, ys5, ssd, big, x, wa, wga, wgb, wc, wo, *, tm=256):
    n = x.shape[0]

    def rows(width, col=0):
        return pl.BlockSpec((tm, width), lambda i: (i, col))

    def full(a):
        return pl.BlockSpec(a.shape, lambda i: (0, 0))

    return pl.pallas_call(
        _merge_kernel,
        grid=(n // tm,),
        in_specs=[rows(ATTN_WIDTH), rows(S5_WIDTH), rows(SSD_INNER),
                  rows(D_MODEL, 2), rows(D_MODEL, 3), rows(D_MODEL, 4), rows(D_MODEL),
                  full(wa), full(wga), full(wgb), full(wc), full(wo)],
        out_specs=rows(D_MODEL),
        out_shape=jax.ShapeDtypeStruct((n, D_MODEL), F32),
        compiler_params=_params("parallel"),
        name="merge",
    )(attn, ys5, ssd, big, big, big, x, wa, wga, wgb, wc, wo)


def _xattn_kernel(x_ref, g_ref, kv_ref, wq_ref, wo_ref, o_ref):
    x = x_ref[...]
    h = (x * lax.rsqrt(jnp.mean(x * x, axis=-1, keepdims=True) + EPS) * g_ref[...]).astype(BF16)
    q = (_dot(h, wq_ref[...]) * (XA_HEAD_DIM ** -0.5)).astype(BF16)
    kv = kv_ref[...]
    k = kv[:, :XA_WIDTH]
    v = kv[:, XA_WIDTH:]
    lane = lax.broadcasted_iota(jnp.int32, k.shape, 1)
    o = None
    for hd in range(XA_HEADS):
        sel = (lane >= hd * XA_HEAD_DIM) & (lane < (hd + 1) * XA_HEAD_DIM)
        kh = jnp.where(sel, k, 0.0).astype(BF16)
        vh = jnp.where(sel, v, 0.0).astype(BF16)
        s = _dot_nt(q, kh)
        s = s - jnp.max(s, axis=-1, keepdims=True)
        p = jnp.exp(s)
        p = p / jnp.sum(p, axis=-1, keepdims=True)
        part = _dot(p.astype(BF16), vh)
        o = part if o is None else o + part
    o_ref[...] = x + _dot(o.astype(BF16), wo_ref[...])


def _xattn(x, g, kv, wq, wo, *, batch, seq, n_mem, tm=256):
    n = batch * seq
    nt = seq // tm
    return pl.pallas_call(
        _xattn_kernel,
        grid=(batch, nt),
        in_specs=[
            pl.BlockSpec((tm, D_MODEL), lambda b, i: (b * nt + i, 0)),
            pl.BlockSpec((1, D_MODEL), lambda b, i: (0, 0)),
            pl.BlockSpec((n_mem, 2 * XA_WIDTH), lambda b, i: (b, 0)),
            pl.BlockSpec((D_MODEL, XA_WIDTH), lambda b, i: (0, 0)),
            pl.BlockSpec((XA_WIDTH, D_MODEL), lambda b, i: (0, 0)),
        ],
        out_specs=pl.BlockSpec((tm, D_MODEL), lambda b, i: (b * nt + i, 0)),
        out_shape=jax.ShapeDtypeStruct((n, D_MODEL), F32),
        compiler_params=_params("parallel", "parallel"),
        name="xattn",
    )(x, g.reshape(1, D_MODEL), kv, wq, wo)


_PEER_CELLS = [(i, j) for i in range(PEER_TOPK) for j in range(PEER_TOPK) if (i + 1) * (j + 1) <= PEER_TOPK]


def _peer_route_kernel(q_ref, k1_ref, k2_ref, s1_ref, s2_ref, e1_ref, e2_ref, tau_ref):
    tm = q_ref.shape[0]
    keys = (k1_ref[...], k2_ref[...])
    kidx = lax.broadcasted_iota(jnp.int32, (PEER_N_KEYS, tm), 0)
    tops = [[None] * PEER_HEADS for _ in range(2)]
    kept = [[None] * PEER_HEADS for _ in range(2)]
    for hd in range(PEER_HEADS):
        for which in range(2):
            c0 = hd * PEER_KEY_DIM + which * PEER_HALF
            st = _dot_nt(keys[which], q_ref[:, c0:c0 + PEER_HALF])
            vals = st
            rows = []
            for _ in range(PEER_TOPK):
                m = jnp.max(vals, axis=0, keepdims=True)
                first = jnp.min(jnp.where(vals == m, kidx, PEER_N_KEYS), axis=0, keepdims=True)
                vals = jnp.where(kidx == first, NEG_INF, vals)
                rows.append(m)
            tops[which][hd] = rows
            kept[which][hd] = jnp.where(vals == NEG_INF, st, NEG_INF)
    v1 = [jnp.concatenate([tops[0][hd][r] for hd in range(PEER_HEADS)], axis=0) for r in range(PEER_TOPK)]
    v2 = [jnp.concatenate([tops[1][hd][r] for hd in range(PEER_HEADS)], axis=0) for r in range(PEER_TOPK)]
    cands = [v1[i] + v2[j] for (i, j) in _PEER_CELLS]
    top_max = v1[0] + v2[0]
    cum = jnp.zeros_like(top_max)
    tau = jnp.full_like(top_max, NEG_INF)
    zsum = jnp.zeros_like(top_max)
    for _ in range(PEER_TOPK):
        m = functools.reduce(jnp.maximum, cands)
        cnt = jnp.zeros_like(top_max)
        nxt = []
        for c in cands:
            eq = c == m
            cnt = cnt + jnp.where(eq, 1.0, 0.0)
            nxt.append(jnp.where(eq, NEG_INF, c))
        cands = nxt
        open_ = cum < PEER_TOPK
        used = jnp.minimum(cnt, PEER_TOPK - cum)
        zsum = zsum + jnp.where(open_, used * jnp.exp(m - top_max), 0.0)
        tau = jnp.where(open_, m, tau)
        cum = cum + cnt
    tau_ref[...] = tau
    inv_z = 1.0 / zsum
    for hd in range(PEER_HEADS):
        s1 = kept[0][hd]
        s2 = kept[1][hd]
        s1_ref[hd] = s1
        s2_ref[hd] = s2
        e1_ref[hd] = jnp.exp(s1 - v1[0][hd:hd + 1, :]) * inv_z[hd:hd + 1, :]
        e2_ref[hd] = jnp.exp(s2 - v2[0][hd:hd + 1, :])


def _peer_route(q, k1, k2, *, tm=256):
    n = q.shape[0]
    big_shape = jax.ShapeDtypeStruct((PEER_HEADS, PEER_N_KEYS, n), F32)
    big_spec = pl.BlockSpec((PEER_HEADS, PEER_N_KEYS, tm), lambda i: (0, 0, i))
    return pl.pallas_call(
        _peer_route_kernel,
        grid=(n // tm,),
        in_specs=[
            pl.BlockSpec((tm, PEER_HEADS * PEER_KEY_DIM), lambda i: (i, 0)),
            pl.BlockSpec((PEER_N_KEYS, PEER_HALF), lambda i: (0, 0)),
            pl.BlockSpec((PEER_N_KEYS, PEER_HALF), lambda i: (0, 0)),
        ],
        out_specs=[big_spec, big_spec, big_spec, big_spec, pl.BlockSpec((PEER_HEADS, tm), lambda i: (0, i))],
        out_shape=[big_shape, big_shape, big_shape, big_shape, jax.ShapeDtypeStruct((PEER_HEADS, n), F32)],
        compiler_params=_params("parallel"),
        name="peer_route",
    )(q, k1, k2)


PEER_TE = 512
PEER_TM = 512


def _peer_dense_kernel(x_ref, h_ref, s1_ref, s2_ref, e1_ref, e2_ref, tau_ref, u_ref, vt_ref, o_ref, acc_ref):
    j = pl.program_id(1)

    @pl.when(j == 0)
    def _():
        acc_ref[...] = jnp.zeros_like(acc_ref)

    act = jax.nn.gelu(_dot_nt(u_ref[...], h_ref[...]))
    parts = []
    for al in range(PEER_TE // PEER_N_KEYS):
        a = j * (PEER_TE // PEER_N_KEYS) + al
        w = None
        for hd in range(PEER_HEADS):
            s1row = s1_ref[hd, pl.ds(a, 1), :]
            e1row = e1_ref[hd, pl.ds(a, 1), :]
            c = s1row + s2_ref[hd]
            contrib = jnp.where(c >= tau_ref[hd:hd + 1, :], e1row * e2_ref[hd], 0.0)
            w = contrib if w is None else w + contrib
        parts.append(w)
    wt = jnp.concatenate(parts, axis=0)
    acc_ref[...] += _dot(vt_ref[...], (wt * act).astype(BF16))

    @pl.when(j == pl.num_programs(1) - 1)
    def _():
        o_ref[...] = x_ref[...] + acc_ref[...].T


def _peer_dense(x, h, s1, s2, e1, e2, tau, u, vt):
    n = x.shape[0]
    tm, te = PEER_TM, PEER_TE
    big_spec = pl.BlockSpec((PEER_HEADS, PEER_N_KEYS, tm), lambda i, j: (0, 0, i))
    return pl.pallas_call(
        _peer_dense_kernel,
        grid=(n // tm, PEER_N_EXPERTS // te),
        in_specs=[
            pl.BlockSpec((tm, D_MODEL), lambda i, j: (i, 0)),
            pl.BlockSpec((tm, D_MODEL), lambda i, j: (i, 0)),
            big_spec, big_spec, big_spec, big_spec,
            pl.BlockSpec((PEER_HEADS, tm), lambda i, j: (0, i)),
            pl.BlockSpec((te, D_MODEL), lambda i, j: (j, 0)),
            pl.BlockSpec((D_MODEL, te), lambda i, j: (0, j)),
        ],
        out_specs=pl.BlockSpec((tm, D_MODEL), lambda i, j: (i, 0)),
        out_shape=jax.ShapeDtypeStruct((n, D_MODEL), F32),
        scratch_shapes=[pltpu.VMEM((D_MODEL, tm), F32)],
        compiler_params=_params("parallel", "arbitrary"),
        name="peer_dense",
    )(x, h, s1, s2, e1, e2, tau, u, vt)


def _final_norm_kernel(x_ref, g_ref, o_ref):
    x = x_ref[...]
    o_ref[...] = x * lax.rsqrt(jnp.mean(x * x, axis=-1, keepdims=True) + EPS) * g_ref[...]


def _final_norm(x, g, *, tm=512):
    n, d = x.shape
    return pl.pallas_call(
        _final_norm_kernel,
        grid=(n // tm,),
        in_specs=[pl.BlockSpec((tm, d), lambda i: (i, 0)), pl.BlockSpec((1, d), lambda i: (0, 0))],
        out_specs=pl.BlockSpec((tm, d), lambda i: (i, 0)),
        out_shape=jax.ShapeDtypeStruct((n, d), F32),
        compiler_params=_params("parallel"),
        name="final_norm",
    )(x, g.reshape(1, d))


def _q_head_perm():
    cols = []
    for j in range(ATTN_HEADS // 2):
        for hd in (j, j + ATTN_HEADS // 2):
            cols.extend(range(hd * ATTN_HEAD_DIM, (hd + 1) * ATTN_HEAD_DIM))
    return np.asarray(cols, dtype=np.int32)


def _split_w_in(w):
    o = 0
    q = w[:, o:o + 512]; o += 512
    k = w[:, o:o + 128]; o += 128
    v = w[:, o:o + 128]; o += 128
    u = w[:, o:o + 512]; o += 512
    z = w[:, o:o + 1024]; o += 1024
    xs = w[:, o:o + 1024]; o += 1024
    bm = w[:, o:o + 128]; o += 128
    cm = w[:, o:o + 128]; o += 128
    dt = w[:, o:o + 16]; o += 16
    gates = w[:, o:o + 3072]
    q = q[:, _q_head_perm()]
    big = jnp.concatenate([z, xs, gates, q, u], axis=1).astype(BF16)
    small = jnp.concatenate([k, v, bm, cm, dt, jnp.zeros((w.shape[0], LANES - SSD_HEADS), w.dtype)], axis=1)
    return big, small.astype(BF16)


def kernel(x, mem, norm_mix, w_in, attn_sinks, s5_lambda_re, s5_lambda_im, s5_log_dt, s5_b_re, s5_b_im, s5_c_re, s5_c_im, s5_d, s5_glu_a, s5_glu_b, ssd_conv_w, ssd_conv_b, ssd_dt_bias, ssd_a_log, ssd_d, ssd_norm, w_attn_out, w_ssd_out, w_o, norm_xattn, norm_mem, xa_wq, xa_wkv, xa_wo, norm_ffn, peer_wq, peer_k1, peer_k2, peer_u, peer_v, norm_final):
    batch, seq, d = x.shape
    n = batch * seq
    n_mem = mem.shape[1]
    depth = w_in.shape[0]
    xf = x.reshape(n, d).astype(F32)
    memf = mem.reshape(batch * n_mem, d).astype(F32)
    for i in range(depth):
        w_big, w_small = _split_w_in(w_in[i])
        big = _norm_mm(xf, norm_mix[i], w_big, tm=1024, tn=512, out_dtype=F32, name="in_proj_big")
        small = _norm_mm(xf, norm_mix[i], w_small, tm=1024, tn=SMALL_COLS, out_dtype=F32, name="in_proj_small")
        attn = _swa(big, small, attn_sinks[i].astype(F32), batch=batch, seq=seq)
        wb, wc, sc = _s5_prepare(s5_lambda_re[i], s5_lambda_im[i], s5_log_dt[i], s5_b_re[i], s5_b_im[i],
                                 s5_c_re[i], s5_c_im[i])
        ys5 = _s5(big, wb, wc, s5_d[i].astype(F32), sc, batch=batch, seq=seq)
        ssd = _ssd(big, small, ssd_conv_w[i], ssd_conv_b[i], ssd_dt_bias[i], ssd_a_log[i], ssd_d[i], ssd_norm[i],
                   batch=batch, seq=seq)
        xf = _merge(attn, ys5, ssd, big, xf,
                    w_attn_out[i][_q_head_perm(), :].astype(BF16), s5_glu_a[i].astype(BF16),
                    s5_glu_b[i].astype(BF16), w_ssd_out[i].astype(BF16), w_o[i].astype(BF16))
        kv = _norm_mm(memf, norm_mem[i], xa_wkv[i].astype(BF16), tm=batch * n_mem, tn=2 * XA_WIDTH,
                      out_dtype=F32, name="mem_kv")
        xf = _xattn(xf, norm_xattn[i], kv, xa_wq[i].astype(BF16), xa_wo[i].astype(BF16),
                    batch=batch, seq=seq, n_mem=n_mem)
        q, h = _norm_mm(xf, norm_ffn[i], peer_wq[i].astype(BF16), tm=1024, tn=512, out_dtype=BF16,
                        with_h=True, name="peer_query")
        s1, s2, e1, e2, tau = _peer_route(q, peer_k1[i].astype(BF16), peer_k2[i].astype(BF16))
        xf = _peer_dense(xf, h, s1, s2, e1, e2, tau, peer_u[i].astype(BF16), peer_v[i].astype(BF16).T)
    return _final_norm(xf, norm_final.astype(F32)).reshape(batch, seq, d)
```

```python
import functools
import math

import jax
import jax.numpy as jnp
import numpy as np
from jax import lax
from jax.experimental import pallas as pl
from jax.experimental.pallas import tpu as pltpu

F32 = jnp.float32
BF16 = jnp.bfloat16
NEG_INF = float("-inf")

D_MODEL = 1024
EPS = 1e-6
ATTN_HEADS = 8
ATTN_HEAD_DIM = 64
ATTN_WIDTH = ATTN_HEADS * ATTN_HEAD_DIM
ATTN_BLOCK = 128
S5_WIDTH = 512
S5_GROUP = 16
S5_GROUPS = S5_WIDTH // S5_GROUP
S5_STATE = 64
S5_LANES = S5_GROUPS * S5_STATE
S5_CHUNK = 256
S5_SCAN_WIDTH = 512
SSD_INNER = 1024
SSD_HEAD_DIM = 64
SSD_HEADS = SSD_INNER // SSD_HEAD_DIM
SSD_STATE = 64
SSD_GROUPS = 2
SSD_CONV = 4
SSD_CHUNK = 128
SSD_BC = SSD_GROUPS * SSD_STATE
SSD_CONV_DIM = SSD_INNER + 2 * SSD_BC
XA_HEADS = 4
XA_HEAD_DIM = 64
XA_WIDTH = XA_HEADS * XA_HEAD_DIM
PEER_HEADS = 8
PEER_KEY_DIM = 256
PEER_HALF = PEER_KEY_DIM // 2
PEER_N_KEYS = 128
PEER_TOPK = 16
PEER_N_EXPERTS = PEER_N_KEYS * PEER_N_KEYS
LANES = 128
SUBLANES = 8
VMEM_LIMIT_BYTES = 56 * 1024 * 1024

BIG_COLS = 6144
SMALL_COLS = 640


def _params(*sem):
    return pltpu.CompilerParams(dimension_semantics=sem, vmem_limit_bytes=VMEM_LIMIT_BYTES)


def _dot(a, b):
    return jnp.dot(a, b, preferred_element_type=F32)


def _dot_nt(a, b):
    return lax.dot_general(a, b, (((1,), (1,)), ((), ())), preferred_element_type=F32)


def _split3(x):
    p1 = x.astype(BF16)
    r1 = x - p1.astype(F32)
    p2 = r1.astype(BF16)
    p3 = (r1 - p2.astype(F32)).astype(BF16)
    return p1, p2, p3


def _exact_dot_rhs(sel, x):
    p1, p2, p3 = _split3(x)
    return _dot(sel, p1) + _dot(sel, p2) + _dot(sel, p3)


def _exact_dot_lhs(x, sel):
    p1, p2, p3 = _split3(x)
    return _dot(p1, sel) + _dot(p2, sel) + _dot(p3, sel)


def _norm_mm_kernel(x_ref, g_ref, w_ref, o_ref, *rest, with_h):
    if with_h:
        hout_ref, h_ref = rest
    else:
        (h_ref,) = rest

    @pl.when(pl.program_id(1) == 0)
    def _():
        x = x_ref[...]
        h = x * lax.rsqrt(jnp.mean(x * x, axis=-1, keepdims=True) + EPS) * g_ref[...]
        h_ref[...] = h.astype(BF16)
        if with_h:
            hout_ref[...] = h.astype(BF16)

    o_ref[...] = _dot(h_ref[...], w_ref[...]).astype(o_ref.dtype)


def _norm_mm(x, g, w, *, tm, tn, out_dtype, with_h=False, name):
    n, k = x.shape
    m = w.shape[1]
    assert n % tm == 0 and m % tn == 0
    out_shape = [jax.ShapeDtypeStruct((n, m), out_dtype)]
    out_specs = [pl.BlockSpec((tm, tn), lambda i, j: (i, j))]
    if with_h:
        out_shape.append(jax.ShapeDtypeStruct((n, k), BF16))
        out_specs.append(pl.BlockSpec((tm, k), lambda i, j: (i, 0)))
    res = pl.pallas_call(
        functools.partial(_norm_mm_kernel, with_h=with_h),
        grid=(n // tm, m // tn),
        in_specs=[
            pl.BlockSpec((tm, k), lambda i, j: (i, 0)),
            pl.BlockSpec((1, k), lambda i, j: (0, 0)),
            pl.BlockSpec((k, tn), lambda i, j: (0, j)),
        ],
        out_specs=out_specs,
        out_shape=out_shape,
        scratch_shapes=[pltpu.VMEM((tm, k), BF16)],
        compiler_params=_params("parallel", "arbitrary"),
        name=name,
    )(x, g.reshape(1, k), w)
    return res if with_h else res[0]


def _swa_kernel(sink_ref, q_ref, kc_ref, kp_ref, vc_ref, vp_ref, o_ref):
    nblk = pl.program_id(1)
    blk = ATTN_BLOCK
    q = q_ref[...]
    k = jnp.concatenate([kp_ref[...], kc_ref[...]], axis=0)
    v = jnp.concatenate([vp_ref[...], vc_ref[...]], axis=0).astype(BF16)
    lane_k = lax.broadcasted_iota(jnp.int32, (2 * blk, LANES), 1)
    k_lo = jnp.where(lane_k < ATTN_HEAD_DIM, k, 0.0).astype(BF16)
    k_hi = jnp.where(lane_k < ATTN_HEAD_DIM, 0.0, k).astype(BF16)
    qi = lax.broadcasted_iota(jnp.int32, (blk, 2 * blk), 0)
    kj = lax.broadcasted_iota(jnp.int32, (blk, 2 * blk), 1)
    dist = qi - kj + blk
    valid = (dist >= 0) & (dist < blk) & ((nblk > 0) | (kj >= blk))
    distf = dist.astype(F32)
    lane_o = lax.broadcasted_iota(jnp.int32, (blk, LANES), 1)
    outs = []
    for j in range(ATTN_HEADS // 2):
        qs = (q[:, j * LANES:(j + 1) * LANES] * (ATTN_HEAD_DIM ** -0.5)).astype(BF16)
        res = []
        for half, kk in ((0, k_lo), (1, k_hi)):
            head = j + 4 * half
            slope = 2.0 ** (-8.0 * (head + 1) / ATTN_HEADS)
            s = _dot_nt(qs, kk) - slope * distf
            s = jnp.where(valid, s, NEG_INF)
            sink = sink_ref[head]
            m = jnp.maximum(jnp.max(s, axis=-1, keepdims=True), sink)
            p = jnp.exp(s - m)
            denom = jnp.sum(p, axis=-1, keepdims=True) + jnp.exp(sink - m)
            p = p / denom
            res.append(_dot(p.astype(BF16), v))
        outs.append(jnp.where(lane_o < ATTN_HEAD_DIM, res[0], res[1]))
    o_ref[...] = jnp.concatenate(outs, axis=1).astype(o_ref.dtype)


def _swa(big, small, sinks, *, batch, seq):
    nb = seq // ATTN_BLOCK
    n = batch * seq
    qcol = 5120 // ATTN_WIDTH

    def cur(col):
        return lambda b, i: (b * nb + i, col)

    def prev(col):
        return lambda b, i: (b * nb + jnp.maximum(i - 1, 0), col)

    return pl.pallas_call(
        _swa_kernel,
        grid=(batch, nb),
        in_specs=[
            pl.BlockSpec(memory_space=pltpu.SMEM),
            pl.BlockSpec((ATTN_BLOCK, ATTN_WIDTH), cur(qcol)),
            pl.BlockSpec((ATTN_BLOCK, LANES), cur(0)),
            pl.BlockSpec((ATTN_BLOCK, LANES), prev(0)),
            pl.BlockSpec((ATTN_BLOCK, LANES), cur(1)),
            pl.BlockSpec((ATTN_BLOCK, LANES), prev(1)),
        ],
        out_specs=pl.BlockSpec((ATTN_BLOCK, ATTN_WIDTH), lambda b, i: (b * nb + i, 0)),
        out_shape=jax.ShapeDtypeStruct((n, ATTN_WIDTH), BF16),
        compiler_params=_params("parallel", "arbitrary"),
        name="swa",
    )(sinks, big, small, small, small, small)


def _s5_kernel(u_ref, wb_ref, wc_ref, d_ref, sc_ref, o_ref, xr_ref, xi_ref, cr_ref, ci_ref):
    @pl.when(pl.program_id(1) == 0)
    def _():
        cr_ref[...] = jnp.zeros_like(cr_ref)
        ci_ref[...] = jnp.zeros_like(ci_ref)

    u = u_ref[...]
    bu = _dot(u.astype(BF16), wb_ref[...])
    xr_ref[...] = bu[:, :S5_LANES]
    xi_ref[...] = bu[:, S5_LANES:]

    def cmul_add(xr, xi, ar, ai, sr, si):
        return xr + (ar * sr - ai * si), xi + (ar * si + ai * sr)

    for w in range(S5_LANES // S5_SCAN_WIDTH):
        sl = slice(w * S5_SCAN_WIDTH, (w + 1) * S5_SCAN_WIDTH)
        consts = [sc_ref[c, :, sl] for c in range(8)]
        a1r, a1i, a2r, a2i, a4r, a4i, pr, pi = consts

        def body(t, carry):
            cr, ci = carry
            r0 = pl.multiple_of(t * SUBLANES, SUBLANES)
            xr = xr_ref[pl.ds(r0, SUBLANES), sl]
            xi = xi_ref[pl.ds(r0, SUBLANES), sl]
            xr, xi = cmul_add(xr, xi, a1r, a1i, pltpu.roll(xr, 1, 0), pltpu.roll(xi, 1, 0))
            xr, xi = cmul_add(xr, xi, a2r, a2i, pltpu.roll(xr, 2, 0), pltpu.roll(xi, 2, 0))
            xr, xi = cmul_add(xr, xi, a4r, a4i, pltpu.roll(xr, 4, 0), pltpu.roll(xi, 4, 0))
            xr, xi = cmul_add(xr, xi, pr, pi, cr, ci)
            xr_ref[pl.ds(r0, SUBLANES), sl] = xr
            xi_ref[pl.ds(r0, SUBLANES), sl] = xi
            return xr[SUBLANES - 1:SUBLANES, :], xi[SUBLANES - 1:SUBLANES, :]

        cr, ci = lax.fori_loop(0, S5_CHUNK // SUBLANES, body, (cr_ref[0:1, sl], ci_ref[0:1, sl]))
        cr_ref[0:1, sl] = cr
        ci_ref[0:1, sl] = ci

    xcat = jnp.concatenate([xr_ref[...].astype(BF16), xi_ref[...].astype(BF16)], axis=1)
    y = _dot(xcat, wc_ref[...]) + d_ref[...] * u
    o_ref[...] = jax.nn.gelu(y).astype(o_ref.dtype)


def _s5_prepare(lam_re, lam_im, log_dt, b_re, b_im, c_re, c_im):
    dt = jnp.exp(log_dt.astype(F32))[:, None]
    lr = lam_re.astype(F32)
    li = lam_im.astype(F32)
    mag = jnp.exp(lr * dt)
    ar = mag * jnp.cos(li * dt)
    ai = mag * jnp.sin(li * dt)
    den = lr * lr + li * li
    nr = ar - 1.0
    wr = (nr * lr + ai * li) / den
    wi = (ai * lr - nr * li) / den
    br_, bi_ = b_re.astype(F32), b_im.astype(F32)
    bbr = wr[..., None] * br_ - wi[..., None] * bi_
    bbi = wr[..., None] * bi_ + wi[..., None] * br_
    eye = jnp.eye(S5_GROUPS, dtype=F32)
    wbr = jnp.einsum('gpi,gh->gihp', bbr, eye).reshape(S5_WIDTH, S5_LANES)
    wbi = jnp.einsum('gpi,gh->gihp', bbi, eye).reshape(S5_WIDTH, S5_LANES)
    wb = jnp.concatenate([wbr, wbi], axis=1).astype(BF16)
    wcr = jnp.einsum('gip,gh->gphi', c_re.astype(F32), eye).reshape(S5_LANES, S5_WIDTH)
    wci = jnp.einsum('gip,gh->gphi', c_im.astype(F32), eye).reshape(S5_LANES, S5_WIDTH)
    wc = jnp.concatenate([wcr, -wci], axis=0).astype(BF16)
    ar_f = ar.reshape(1, S5_LANES)
    ai_f = ai.reshape(1, S5_LANES)

    def cmul(xr, xi, yr, yi):
        return xr * yr - xi * yi, xr * yi + xi * yr

    pows_r, pows_i = [ar_f], [ai_f]
    for _ in range(SUBLANES - 1):
        nr_, ni_ = cmul(pows_r[-1], pows_i[-1], ar_f, ai_f)
        pows_r.append(nr_)
        pows_i.append(ni_)
    t_idx = jnp.arange(SUBLANES)[:, None]

    def masked(k):
        return (jnp.where(t_idx >= k, pows_r[k - 1], 0.0), jnp.where(t_idx >= k, pows_i[k - 1], 0.0))

    a1 = masked(1)
    a2 = masked(2)
    a4 = masked(4)
    pr = jnp.concatenate(pows_r, axis=0)
    pi = jnp.concatenate(pows_i, axis=0)
    sc = jnp.stack([a1[0], a1[1], a2[0], a2[1], a4[0], a4[1], pr, pi], axis=0)
    return wb, wc, sc


def _s5(big, wb, wc, d, sc, *, batch, seq):
    n = batch * seq
    nc = seq // S5_CHUNK
    ucol = 5632 // S5_WIDTH
    return pl.pallas_call(
        _s5_kernel,
        grid=(batch, nc),
        in_specs=[
            pl.BlockSpec((S5_CHUNK, S5_WIDTH), lambda b, c: (b * nc + c, ucol)),
            pl.BlockSpec((S5_WIDTH, 2 * S5_LANES), lambda b, c: (0, 0)),
            pl.BlockSpec((2 * S5_LANES, S5_WIDTH), lambda b, c: (0, 0)),
            pl.BlockSpec((1, S5_WIDTH), lambda b, c: (0, 0)),
            pl.BlockSpec((8, SUBLANES, S5_LANES), lambda b, c: (0, 0, 0)),
        ],
        out_specs=pl.BlockSpec((S5_CHUNK, S5_WIDTH), lambda b, c: (b * nc + c, 0)),
        out_shape=jax.ShapeDtypeStruct((n, S5_WIDTH), BF16),
        scratch_shapes=[
            pltpu.VMEM((S5_CHUNK, S5_LANES), F32),
            pltpu.VMEM((S5_CHUNK, S5_LANES), F32),
            pltpu.VMEM((SUBLANES, S5_LANES), F32),
            pltpu.VMEM((SUBLANES, S5_LANES), F32),
        ],
        compiler_params=_params("parallel", "arbitrary"),
        name="s5",
    )(big, wb, wc, d.reshape(1, S5_WIDTH), sc)


def _ssd_kernel(z_ref, xs_ref, b_ref, c_ref, dt_ref, cw_ref, cb_ref, dtb_ref, alog_ref, dvec_ref, nw_ref,
                exp_h_ref, exp_l_ref, o_ref, buf_ref, st_ref):
    q = SSD_CHUNK

    @pl.when(pl.program_id(1) == 0)
    def _():
        buf_ref[0:SUBLANES, :] = jnp.zeros((SUBLANES, SSD_CONV_DIM), F32)
        st_ref[...] = jnp.zeros_like(st_ref)

    buf_ref[SUBLANES:SUBLANES + q, 0:SSD_INNER] = xs_ref[...]
    buf_ref[SUBLANES:SUBLANES + q, SSD_INNER:SSD_INNER + SSD_BC] = b_ref[...]
    buf_ref[SUBLANES:SUBLANES + q, SSD_INNER + SSD_BC:SSD_CONV_DIM] = c_ref[...]
    acc = jnp.broadcast_to(cb_ref[...], (q, SSD_CONV_DIM))
    for k in range(SSD_CONV):
        acc = acc + cw_ref[k:k + 1, :] * buf_ref[pl.ds(SUBLANES - (SSD_CONV - 1) + k, q), :]
    buf_ref[0:SUBLANES, :] = buf_ref[q:q + SUBLANES, :]
    xbc = acc * jax.nn.sigmoid(acc)
    xs = xbc[:, :SSD_INNER]
    bm = xbc[:, SSD_INNER:SSD_INNER + SSD_BC]
    cm = xbc[:, SSD_INNER + SSD_BC:]

    lane = lax.broadcasted_iota(jnp.int32, (q, LANES), 1)
    row = lax.broadcasted_iota(jnp.int32, (q, LANES), 0)
    head_lane = lane < SSD_HEADS
    dt_in = dt_ref[...] + dtb_ref[...]
    dt = jnp.maximum(dt_in, 0.0) + jnp.log1p(jnp.exp(-jnp.abs(dt_in)))
    dt = jnp.where(head_lane, dt, 0.0)
    a = -jnp.exp(alog_ref[...])
    ad = dt * a
    tri = (row >= lane).astype(BF16)
    cs = _exact_dot_rhs(tri, ad)
    cs_t = cs.T
    dt_t = dt.T
    total = cs[q - 1:q, :]
    dec_t = jnp.exp(cs_t[:, q - 1:q] - cs_t)
    w_t = dec_t * dt_t

    exp_h = exp_h_ref[...]
    exp_l = exp_l_ref[...]
    ecs_x = _exact_dot_lhs(jnp.exp(cs), exp_h)
    w_x = _exact_dot_lhs(w_t.T, exp_h)
    cs_cols = _exact_dot_lhs(cs, exp_l)
    tot_x = _exact_dot_lhs(jnp.broadcast_to(jnp.exp(total), (SUBLANES, LANES)), exp_h)[0:1, :]

    bm16 = bm.astype(BF16)
    cm16 = cm.astype(BF16)
    xs16 = xs.astype(BF16)
    low = lane < SSD_STATE
    g_mats = [_dot_nt(jnp.where(low, cm, 0.0).astype(BF16), bm16),
              _dot_nt(jnp.where(low, 0.0, cm).astype(BF16), bm16)]
    causal = row >= lane
    ydiag = []
    for j in range(SSD_HEADS // 2):
        slab = xs[:, j * LANES:(j + 1) * LANES]
        halves = (jnp.where(low, slab, 0.0).astype(BF16), jnp.where(low, 0.0, slab).astype(BF16))
        acc_j = None
        for half in range(2):
            h = 2 * j + half
            g = h // (SSD_HEADS // SSD_GROUPS)
            seg = cs_cols[:, h * LANES:(h + 1) * LANES] - cs_t[h:h + 1, :]
            lmat = jnp.exp(jnp.where(causal, seg, NEG_INF))
            m = (g_mats[g] * lmat * dt_t[h:h + 1, :]).astype(BF16)
            part = _dot(m, halves[half])
            acc_j = part if acc_j is None else acc_j + part
        ydiag.append(acc_j)
    y = jnp.concatenate(ydiag, axis=1)

    st = st_ref[...]
    y = y + _dot(cm16, st.astype(BF16)) * ecs_x
    new = _dot(bm.T.astype(BF16), (xs * w_x).astype(BF16))
    srow = lax.broadcasted_iota(jnp.int32, (LANES, SSD_INNER), 0)
    scol = lax.broadcasted_iota(jnp.int32, (LANES, SSD_INNER), 1)
    same_group = (srow < SSD_STATE) == (scol < SSD_INNER // SSD_GROUPS)
    st_ref[...] = jnp.where(same_group, st * tot_x + new, 0.0)

    y = y + dvec_ref[...] * xs
    zz = z_ref[...]
    gated = y * (zz * jax.nn.sigmoid(zz))
    out = gated * lax.rsqrt(jnp.mean(gated * gated, axis=-1, keepdims=True) + EPS) * nw_ref[...]
    o_ref[...] = out.astype(o_ref.dtype)


def _ssd(big, small, conv_w, conv_b, dt_bias, a_log, dvec, norm_w, *, batch, seq):
    n = batch * seq
    nc = seq // SSD_CHUNK
    q = SSD_CHUNK
    pad = LANES - SSD_HEADS
    dtb = jnp.pad(dt_bias.astype(F32), (0, pad)).reshape(1, LANES)
    alog = jnp.pad(a_log.astype(F32), (0, pad)).reshape(1, LANES)
    d_x = jnp.repeat(dvec.astype(F32), SSD_HEAD_DIM).reshape(1, SSD_INNER)
    heads = np.arange(LANES)[:, None]
    exp_h = jnp.asarray(heads == (np.arange(SSD_INNER)[None, :] // SSD_HEAD_DIM), dtype=BF16)
    exp_l = jnp.asarray(heads == (np.arange(SSD_HEADS * LANES)[None, :] // LANES), dtype=BF16)

    def blk(col, width):
        return pl.BlockSpec((q, width), lambda b, c: (b * nc + c, col))

    def full(shape):
        return pl.BlockSpec(shape, lambda b, c: (0,) * len(shape))

    return pl.pallas_call(
        _ssd_kernel,
        grid=(batch, nc),
        in_specs=[
            blk(0, SSD_INNER), blk(1, SSD_INNER),
            blk(2, LANES), blk(3, LANES), blk(4, LANES),
            full((SSD_CONV, SSD_CONV_DIM)), full((1, SSD_CONV_DIM)),
            full((1, LANES)), full((1, LANES)), full((1, SSD_INNER)), full((1, SSD_INNER)),
            full((LANES, SSD_INNER)), full((LANES, SSD_HEADS * LANES)),
        ],
        out_specs=pl.BlockSpec((q, SSD_INNER), lambda b, c: (b * nc + c, 0)),
        out_shape=jax.ShapeDtypeStruct((n, SSD_INNER), BF16),
        scratch_shapes=[
            pltpu.VMEM((q + SUBLANES, SSD_CONV_DIM), F32),
            pltpu.VMEM((LANES, SSD_INNER), F32),
        ],
        compiler_params=_params("parallel", "arbitrary"),
        name="ssd",
    )(big, big, small, small, small, conv_w.astype(F32), conv_b.astype(F32).reshape(1, SSD_CONV_DIM),
      dtb, alog, d_x, norm_w.astype(F32).reshape(1, SSD_INNER), exp_h, exp_l)


def _merge_kernel(attn_ref, ys5_ref, ssd_ref, ga_ref, gb_ref, gc_ref, x_ref,
                  wa_ref, wga_ref, wgb_ref, wc_ref, wo_ref, o_ref):
    ys5 = ys5_ref[...]
    br_a = _dot(attn_ref[...], wa_ref[...])
    br_b = _dot(ys5, wga_ref[...]) * jax.nn.sigmoid(_dot(ys5, wgb_ref[...]))
    br_c = _dot(ssd_ref[...], wc_ref[...])
    merged = (jax.nn.sigmoid(ga_ref[...]) * br_a + jax.nn.sigmoid(gb_ref[...]) * br_b
              + jax.nn.sigmoid(gc_ref[...]) * br_c)
    o_ref[...] = x_ref[...] + _dot(merged.astype(BF16), wo_ref[...])


def _merge(attn, ys5, ssd, big, x, wa, wga, wgb, wc, wo, *, tm=256):
    n = x.shape[0]

    def rows(width, col=0):
        return pl.BlockSpec((tm, width), lambda i: (i, col))

    def full(a):
        return pl.BlockSpec(a.shape, lambda i: (0, 0))

    return pl.pallas_call(
        _merge_kernel,
        grid=(n // tm,),
        in_specs=[rows(ATTN_WIDTH), rows(S5_WIDTH), rows(SSD_INNER),
                  rows(D_MODEL, 2), rows(D_MODEL, 3), rows(D_MODEL, 4), rows(D_MODEL),
                  full(wa), full(wga), full(wgb), full(wc), full(wo)],
        out_specs=rows(D_MODEL),
        out_shape=jax.ShapeDtypeStruct((n, D_MODEL), F32),
        compiler_params=_params("parallel"),
        name="merge",
    )(attn, ys5, ssd, big, big, big, x, wa, wga, wgb, wc, wo)


def _xattn_kernel(x_ref, g_ref, kv_ref, wq_ref, wo_ref, o_ref):
    x = x_ref[...]
    h = (x * lax.rsqrt(jnp.mean(x * x, axis=-1, keepdims=True) + EPS) * g_ref[...]).astype(BF16)
    q = (_dot(h, wq_ref[...]) * (XA_HEAD_DIM ** -0.5)).astype(BF16)
    kv = kv_ref[...]
    k = kv[:, :XA_WIDTH]
    v = kv[:, XA_WIDTH:]
    lane = lax.broadcasted_iota(jnp.int32, k.shape, 1)
    o = None
    for hd in range(XA_HEADS):
        sel = (lane >= hd * XA_HEAD_DIM) & (lane < (hd + 1) * XA_HEAD_DIM)
        kh = jnp.where(sel, k, 0.0).astype(BF16)
        vh = jnp.where(sel, v, 0.0).astype(BF16)
        s = _dot_nt(q, kh)
        s = s - jnp.max(s, axis=-1, keepdims=True)
        p = jnp.exp(s)
        p = p / jnp.sum(p, axis=-1, keepdims=True)
        part = _dot(p.astype(BF16), vh)
        o = part if o is None else o + part
    o_ref[...] = x + _dot(o.astype(BF16), wo_ref[...])


def _xattn(x, g, kv, wq, wo, *, batch, seq, n_mem, tm=256):
    n = batch * seq
    nt = seq // tm
    return pl.pallas_call(
        _xattn_kernel,
        grid=(batch, nt),
        in_specs=[
            pl.BlockSpec((tm, D_MODEL), lambda b, i: (b * nt + i, 0)),
            pl.BlockSpec((1, D_MODEL), lambda b, i: (0, 0)),
            pl.BlockSpec((n_mem, 2 * XA_WIDTH), lambda b, i: (b, 0)),
            pl.BlockSpec((D_MODEL, XA_WIDTH), lambda b, i: (0, 0)),
            pl.BlockSpec((XA_WIDTH, D_MODEL), lambda b, i: (0, 0)),
        ],
        out_specs=pl.BlockSpec((tm, D_MODEL), lambda b, i: (b * nt + i, 0)),
        out_shape=jax.ShapeDtypeStruct((n, D_MODEL), F32),
        compiler_params=_params("parallel", "parallel"),
        name="xattn",
    )(x, g.reshape(1, D_MODEL), kv, wq, wo)


_PEER_CELLS = [(i, j) for i in range(PEER_TOPK) for j in range(PEER_TOPK) if (i + 1) * (j + 1) <= PEER_TOPK]


def _peer_route_kernel(q_ref, k1_ref, k2_ref, s1_ref, s2_ref, tau_ref):
    tm = q_ref.shape[0]
    keys = (k1_ref[...], k2_ref[...])
    kidx = lax.broadcasted_iota(jnp.int32, (PEER_N_KEYS, tm), 0).astype(F32)
    tops =[[None] * PEER_HEADS for _ in range(2)]
    kept = [[None] * PEER_HEADS for _ in range(2)]
    for hd in range(PEER_HEADS):
        for which in range(2):
            c0 = hd * PEER_KEY_DIM + which * PEER_HALF
            st = _dot_nt(keys[which], q_ref[:, c0:c0 + PEER_HALF])
            vals = st
            rows = []
            for _ in range(PEER_TOPK):
                m = jnp.max(vals, axis=0, keepdims=True)
                first = jnp.min(jnp.where(vals == m, kidx, float(PEER_N_KEYS)), axis=0, keepdims=True)
                vals = jnp.where(kidx == first, NEG_INF, vals)
                rows.append(m)
            tops[which][hd] = [r - rows[0] for r in rows]
            kept[which][hd] = jnp.where(vals == NEG_INF, st - rows[0], NEG_INF)
    v1 = [jnp.concatenate([tops[0][hd][r] for hd in range(PEER_HEADS)], axis=0) for r in range(PEER_TOPK)]
    v2 = [jnp.concatenate([tops[1][hd][r] for hd in range(PEER_HEADS)], axis=0) for r in range(PEER_TOPK)]
    cands = [v1[i] + v2[j] for (i, j) in _PEER_CELLS]
    work = cands
    cum = jnp.zeros_like(v1[0])
    tau = jnp.full_like(v1[0], NEG_INF)
    zsum = jnp.zeros_like(v1[0])
    for _ in range(PEER_TOPK):
        m = functools.reduce(jnp.maximum, work)
        cnt = jnp.zeros_like(m)
        nxt = []
        for c in work:
            eq = c == m
            cnt = cnt + jnp.where(eq, 1.0, 0.0)
            nxt.append(jnp.where(eq, NEG_INF, c))
        work = nxt
        open_ = cum < PEER_TOPK
        used = jnp.minimum(cnt, PEER_TOPK - cum)
        zsum = zsum + jnp.where(open_, used * jnp.exp(m), 0.0)
        tau = jnp.where(open_, m, tau)
        cum = cum + cnt
    log_z = jnp.log(zsum)
    tau_f = jnp.full_like(tau, jnp.inf)
    for (i, j), c in zip(_PEER_CELLS, cands):
        tau_f = jnp.minimum(tau_f, jnp.where(c >= tau, (v1[i] - log_z) + v2[j], jnp.inf))
    tau_ref[0] = tau_f
    for hd in range(PEER_HEADS):
        s1_ref[hd, 0] = kept[0][hd] - log_z[hd:hd + 1, :]
        s2_ref[hd, 0] = kept[1][hd]


def _peer_route(q, k1, k2):
    n = q.shape[0]
    tm = LANES
    ns = n // tm
    big_shape = jax.ShapeDtypeStruct((PEER_HEADS, ns, PEER_N_KEYS, tm), F32)
    big_spec = pl.BlockSpec((PEER_HEADS, 1, PEER_N_KEYS, tm), lambda i: (0, i, 0, 0))
    return pl.pallas_call(
        _peer_route_kernel,
        grid=(ns,),
        in_specs=[
            pl.BlockSpec((tm, PEER_HEADS * PEER_KEY_DIM), lambda i: (i, 0)),
            pl.BlockSpec((PEER_N_KEYS, PEER_HALF), lambda i: (0, 0)),
            pl.BlockSpec((PEER_N_KEYS, PEER_HALF), lambda i: (0, 0)),
        ],
        out_specs=[big_spec, big_spec, pl.BlockSpec((1, PEER_HEADS, tm), lambda i: (i, 0, 0))],
        out_shape=[big_shape, big_shape, jax.ShapeDtypeStruct((ns, PEER_HEADS, tm), F32)],
        compiler_params=_params("parallel"),
        name="peer_route",
    )(q, k1, k2)


PEER_TE = 512
PEER_TM = 1024
PEER_STRIP = 256


def _peer_dense_kernel(x_ref, h_ref, s1_ref, s2_ref, tau_ref, u_ref, vt_ref, o_ref, acc_ref, act_ref, p_ref):
    j = pl.program_id(1)
    n_slab = PEER_TE // PEER_N_KEYS

    @pl.when(j == 0)
    def _():
        acc_ref[...] = jnp.zeros_like(acc_ref)

    u = u_ref[...]
    vt = vt_ref[...]
    wide = PEER_STRIP
    n_strip = PEER_TM // wide

    def first_matmul(s):
        r0 = pl.multiple_of(s * wide, wide)
        return _dot_nt(u, h_ref[pl.ds(r0, wide), :])

    act_ref[0] = first_matmul(0)
    p_ref[1] = jnp.zeros(p_ref.shape[1:], BF16)

    def one_strip(s, slot):
        act_ref[1 - slot] = first_matmul(jnp.minimum(s + 1, n_strip - 1))
        acc_ref[jnp.maximum(s - 1, 0)] += _dot(vt, p_ref[1 - slot])
        for half in range(wide // LANES):
            tc = s * (wide // LANES) + half
            tau = tau_ref[tc]
            for al in range(n_slab):
                a = j * n_slab + al
                w = None
                for hd in range(PEER_HEADS):
                    c = s1_ref[hd, tc, pl.ds(a, 1), :] + s2_ref[hd, tc]
                    contrib = jnp.where(c >= tau[hd:hd + 1, :], jnp.exp(c), 0.0)
                    w = contrib if w is None else w + contrib
                rows = slice(al * PEER_N_KEYS, (al + 1) * PEER_N_KEYS)
                lanes = slice(half * LANES, (half + 1) * LANES)
                g = jax.nn.gelu(act_ref[slot, rows, lanes])
                p_ref[slot, rows, lanes] = (w * g).astype(BF16)

    def pair(k, carry):
        one_strip(2 * k, 0)
        one_strip(2 * k + 1, 1)
        return carry

    lax.fori_loop(0, n_strip // 2, pair, 0)
    acc_ref[n_strip - 1] += _dot(vt, p_ref[(n_strip - 1) % 2])

    @pl.when(j == pl.num_programs(1) - 1)
    def _():
        for s in range(n_strip):
            o_ref[s * wide:(s + 1) * wide, :] = x_ref[s * wide:(s + 1) * wide, :] + acc_ref[s].T


def _peer_dense(x, h, s1, s2, tau, u, vt):
    n = x.shape[0]
    tm, te = PEER_TM, PEER_TE
    ns = tm // LANES
    big_spec = pl.BlockSpec((PEER_HEADS, ns, PEER_N_KEYS, LANES), lambda i, j: (0, i, 0, 0))
    return pl.pallas_call(
        _peer_dense_kernel,
        grid=(n // tm, PEER_N_EXPERTS // te),
        in_specs=[
            pl.BlockSpec((tm, D_MODEL), lambda i, j: (i, 0)),
            pl.BlockSpec((tm, D_MODEL), lambda i, j: (i, 0)),
            big_spec, big_spec,
            pl.BlockSpec((ns, PEER_HEADS, LANES), lambda i, j: (i, 0, 0)),
            pl.BlockSpec((te, D_MODEL), lambda i, j: (j, 0)),
            pl.BlockSpec((D_MODEL, te), lambda i, j: (0, j)),
        ],
        out_specs=pl.BlockSpec((tm, D_MODEL), lambda i, j: (i, 0)),
        out_shape=jax.ShapeDtypeStruct((n, D_MODEL), F32),
        scratch_shapes=[
            pltpu.VMEM((tm // PEER_STRIP, D_MODEL, PEER_STRIP), F32),
            pltpu.VMEM((2, te, PEER_STRIP), F32),
            pltpu.VMEM((2, te, PEER_STRIP), BF16),
        ],
        compiler_params=_params("parallel", "arbitrary"),
        name="peer_dense",
    )(x, h, s1, s2, tau, u, vt)


def _final_norm_kernel(x_ref, g_ref, o_ref):
    x = x_ref[...]
    o_ref[...] = x * lax.rsqrt(jnp.mean(x * x, axis=-1, keepdims=True) + EPS) * g_ref[...]


def _final_norm(x, g, *, tm=512):
    n, d = x.shape
    return pl.pallas_call(
        _final_norm_kernel,
        grid=(n // tm,),
        in_specs=[pl.BlockSpec((tm, d), lambda i: (i, 0)), pl.BlockSpec((1, d), lambda i: (0, 0))],
        out_specs=pl.BlockSpec((tm, d), lambda i: (i, 0)),
        out_shape=jax.ShapeDtypeStruct((n, d), F32),
        compiler_params=_params("parallel"),
        name="final_norm",
    )(x, g.reshape(1, d))


def _q_head_perm():
    cols = []
    for j in range(ATTN_HEADS // 2):
        for hd in (j, j + ATTN_HEADS // 2):
            cols.extend(range(hd * ATTN_HEAD_DIM, (hd + 1) * ATTN_HEAD_DIM))
    return np.asarray(cols, dtype=np.int32)


def _split_w_in(w):
    o = 0
    q = w[:, o:o + 512]; o += 512
    k = w[:, o:o + 128]; o += 128
    v = w[:, o:o + 128]; o += 128
    u = w[:, o:o + 512]; o += 512
    z = w[:, o:o + 1024]; o += 1024
    xs = w[:, o:o + 1024]; o += 1024
    bm = w[:, o:o + 128]; o += 128
    cm = w[:, o:o + 128]; o += 128
    dt = w[:, o:o + 16]; o += 16
    gates = w[:, o:o + 3072]
    q = q[:, _q_head_perm()]
    big = jnp.concatenate([z, xs, gates, q, u], axis=1).astype(BF16)
    small = jnp.concatenate([k, v, bm, cm, dt, jnp.zeros((w.shape[0], LANES - SSD_HEADS), w.dtype)], axis=1)
    return big, small.astype(BF16)


def kernel(x, mem, norm_mix, w_in, attn_sinks, s5_lambda_re, s5_lambda_im, s5_log_dt, s5_b_re, s5_b_im, s5_c_re, s5_c_im, s5_d, s5_glu_a, s5_glu_b, ssd_conv_w, ssd_conv_b, ssd_dt_bias, ssd_a_log, ssd_d, ssd_norm, w_attn_out, w_ssd_out, w_o, norm_xattn, norm_mem, xa_wq, xa_wkv, xa_wo, norm_ffn, peer_wq, peer_k1, peer_k2, peer_u, peer_v, norm_final):
    batch, seq, d = x.shape
    n = batch * seq
    n_mem = mem.shape[1]
    depth = w_in.shape[0]
    xf = x.reshape(n, d).astype(F32)
    memf = mem.reshape(batch * n_mem, d).astype(F32)
    for i in range(depth):
        w_big, w_small = _split_w_in(w_in[i])
        big = _norm_mm(xf, norm_mix[i], w_big, tm=1024, tn=512, out_dtype=F32, name="in_proj_big")
        small = _norm_mm(xf, norm_mix[i], w_small, tm=1024, tn=SMALL_COLS, out_dtype=F32, name="in_proj_small")
        attn = _swa(big, small, attn_sinks[i].astype(F32), batch=batch, seq=seq)
        wb, wc, sc = _s5_prepare(s5_lambda_re[i], s5_lambda_im[i], s5_log_dt[i], s5_b_re[i], s5_b_im[i],
                                 s5_c_re[i], s5_c_im[i])
        ys5 = _s5(big, wb, wc, s5_d[i].astype(F32), sc, batch=batch, seq=seq)
        ssd = _ssd(big, small, ssd_conv_w[i], ssd_conv_b[i], ssd_dt_bias[i], ssd_a_log[i], ssd_d[i], ssd_norm[i],
                   batch=batch, seq=seq)
        xf = _merge(attn, ys5, ssd, big, xf,
                    w_attn_out[i][_q_head_perm(), :].astype(BF16), s5_glu_a[i].astype(BF16),
                    s5_glu_b[i].astype(BF16), w_ssd_out[i].astype(BF16), w_o[i].astype(BF16))
        kv = _norm_mm(memf, norm_mem[i], xa_wkv[i].astype(BF16), tm=batch * n_mem, tn=2 * XA_WIDTH,
                      out_dtype=F32, name="mem_kv")
        xf = _xattn(xf, norm_xattn[i], kv, xa_wq[i].astype(BF16), xa_wo[i].astype(BF16),
                    batch=batch, seq=seq, n_mem=n_mem)
        q, h = _norm_mm(xf, norm_ffn[i], peer_wq[i].astype(BF16), tm=1024, tn=512, out_dtype=BF16,
                        with_h=True, name="peer_query")
        s1, s2, tau = _peer_route(q, peer_k1[i].astype(BF16), peer_k2[i].astype(BF16))
        xf = _peer_dense(xf, h, s1, s2, tau, peer_u[i].astype(BF16), peer_v[i].astype(BF16).T)
    return _final_norm(xf, norm_final.astype(F32)).reshape(batch, seq, d)
```

```python
import functools
import math

import jax
import jax.numpy as jnp
import numpy as np
from jax import lax
from jax.experimental import pallas as pl
from jax.experimental.pallas import tpu as pltpu

F32 = jnp.float32
BF16 = jnp.bfloat16
NEG_INF = float("-inf")

D_MODEL = 1024
EPS = 1e-6
ATTN_HEADS = 8
ATTN_HEAD_DIM = 64
ATTN_WIDTH = ATTN_HEADS * ATTN_HEAD_DIM
ATTN_BLOCK = 128
S5_WIDTH = 512
S5_GROUP = 16
S5_GROUPS = S5_WIDTH // S5_GROUP
S5_STATE = 64
S5_LANES = S5_GROUPS * S5_STATE
S5_CHUNK = 256
S5_SCAN_WIDTH = 512
SSD_INNER = 1024
SSD_HEAD_DIM = 64
SSD_HEADS = SSD_INNER // SSD_HEAD_DIM
SSD_STATE = 64
SSD_GROUPS = 2
SSD_CONV = 4
SSD_CHUNK = 128
SSD_BC = SSD_GROUPS * SSD_STATE
SSD_CONV_DIM = SSD_INNER + 2 * SSD_BC
XA_HEADS = 4
XA_HEAD_DIM = 64
XA_WIDTH = XA_HEADS * XA_HEAD_DIM
PEER_HEADS = 8
PEER_KEY_DIM = 256
PEER_HALF = PEER_KEY_DIM // 2
PEER_N_KEYS = 128
PEER_TOPK = 16
PEER_N_EXPERTS = PEER_N_KEYS * PEER_N_KEYS
LANES = 128
SUBLANES = 8
VMEM_LIMIT_BYTES = 56 * 1024 * 1024

BIG_COLS = 6144
SMALL_COLS = 640


def _params(*sem):
    return pltpu.CompilerParams(dimension_semantics=sem, vmem_limit_bytes=VMEM_LIMIT_BYTES)


def _dot(a, b):
    return jnp.dot(a, b, preferred_element_type=F32)


def _dot_nt(a, b):
    return lax.dot_general(a, b, (((1,), (1,)), ((), ())), preferred_element_type=F32)


def _split3(x):
    p1 = x.astype(BF16)
    r1 = x - p1.astype(F32)
    p2 = r1.astype(BF16)
    p3 = (r1 - p2.astype(F32)).astype(BF16)
    return p1, p2, p3


def _exact_dot_rhs(sel, x):
    p1, p2, p3 = _split3(x)
    return _dot(sel, p1) + _dot(sel, p2) + _dot(sel, p3)


def _exact_dot_lhs(x, sel):
    p1, p2, p3 = _split3(x)
    return _dot(p1, sel) + _dot(p2, sel) + _dot(p3, sel)


def _norm_mm_kernel(x_ref, g_ref, w_ref, o_ref, *rest, with_h):
    if with_h:
        hout_ref, h_ref = rest
    else:
        (h_ref,) = rest

    @pl.when(pl.program_id(1) == 0)
    def _():
        x = x_ref[...]
        h = x * lax.rsqrt(jnp.mean(x * x, axis=-1, keepdims=True) + EPS) * g_ref[...]
        h_ref[...] = h.astype(BF16)
        if with_h:
            hout_ref[...] = h.astype(BF16)

    o_ref[...] = _dot(h_ref[...], w_ref[...]).astype(o_ref.dtype)


def _norm_mm(x, g, w, *, tm, tn, out_dtype, with_h=False, name):
    n, k = x.shape
    m = w.shape[1]
    assert n % tm == 0 and m % tn == 0
    out_shape = [jax.ShapeDtypeStruct((n, m), out_dtype)]
    out_specs = [pl.BlockSpec((tm, tn), lambda i, j: (i, j))]
    if with_h:
        out_shape.append(jax.ShapeDtypeStruct((n, k), BF16))
        out_specs.append(pl.BlockSpec((tm, k), lambda i, j: (i, 0)))
    res = pl.pallas_call(
        functools.partial(_norm_mm_kernel, with_h=with_h),
        grid=(n // tm, m // tn),
        in_specs=[
            pl.BlockSpec((tm, k), lambda i, j: (i, 0)),
            pl.BlockSpec((1, k), lambda i, j: (0, 0)),
            pl.BlockSpec((k, tn), lambda i, j: (0, j)),
        ],
        out_specs=out_specs,
        out_shape=out_shape,
        scratch_shapes=[pltpu.VMEM((tm, k), BF16)],
        compiler_params=_params("parallel", "arbitrary"),
        name=name,
    )(x, g.reshape(1, k), w)
    return res if with_h else res[0]


def _swa_kernel(sink_ref, q_ref, kc_ref, kp_ref, vc_ref, vp_ref, o_ref):
    nblk = pl.program_id(1)
    blk = ATTN_BLOCK
    q = q_ref[...]
    k = jnp.concatenate([kp_ref[...], kc_ref[...]], axis=0)
    v = jnp.concatenate([vp_ref[...], vc_ref[...]], axis=0).astype(BF16)
    lane_k = lax.broadcasted_iota(jnp.int32, (2 * blk, LANES), 1)
    k_lo = jnp.where(lane_k < ATTN_HEAD_DIM, k, 0.0).astype(BF16)
    k_hi = jnp.where(lane_k < ATTN_HEAD_DIM, 0.0, k).astype(BF16)
    qi = lax.broadcasted_iota(jnp.int32, (blk, 2 * blk), 0)
    kj = lax.broadcasted_iota(jnp.int32, (blk, 2 * blk), 1)
    dist = qi - kj + blk
    valid = (dist >= 0) & (dist < blk) & ((nblk > 0) | (kj >= blk))
    distf = dist.astype(F32)
    lane_o = lax.broadcasted_iota(jnp.int32, (blk, LANES), 1)
    outs = []
    for j in range(ATTN_HEADS // 2):
        qs = (q[:, j * LANES:(j + 1) * LANES] * (ATTN_HEAD_DIM ** -0.5)).astype(BF16)
        res = []
        for half, kk in ((0, k_lo), (1, k_hi)):
            head = j + 4 * half
            slope = 2.0 ** (-8.0 * (head + 1) / ATTN_HEADS)
            s = _dot_nt(qs, kk) - slope * distf
            s = jnp.where(valid, s, NEG_INF)
            sink = sink_ref[head]
            m = jnp.maximum(jnp.max(s, axis=-1, keepdims=True), sink)
            p = jnp.exp(s - m)
            denom = jnp.sum(p, axis=-1, keepdims=True) + jnp.exp(sink - m)
            p = p / denom
            res.append(_dot(p.astype(BF16), v))
        outs.append(jnp.where(lane_o < ATTN_HEAD_DIM, res[0], res[1]))
    o_ref[...] = jnp.concatenate(outs, axis=1).astype(o_ref.dtype)


def _swa(big, small, sinks, *, batch, seq):
    nb = seq // ATTN_BLOCK
    n = batch * seq
    qcol = 5120 // ATTN_WIDTH

    def cur(col):
        return lambda b, i: (b * nb + i, col)

    def prev(col):
        return lambda b, i: (b * nb + jnp.maximum(i - 1, 0), col)

    return pl.pallas_call(
        _swa_kernel,
        grid=(batch, nb),
        in_specs=[
            pl.BlockSpec(memory_space=pltpu.SMEM),
            pl.BlockSpec((ATTN_BLOCK, ATTN_WIDTH), cur(qcol)),
            pl.BlockSpec((ATTN_BLOCK, LANES), cur(0)),
            pl.BlockSpec((ATTN_BLOCK, LANES), prev(0)),
            pl.BlockSpec((ATTN_BLOCK, LANES), cur(1)),
            pl.BlockSpec((ATTN_BLOCK, LANES), prev(1)),
        ],
        out_specs=pl.BlockSpec((ATTN_BLOCK, ATTN_WIDTH), lambda b, i: (b * nb + i, 0)),
        out_shape=jax.ShapeDtypeStruct((n, ATTN_WIDTH), BF16),
        compiler_params=_params("parallel", "arbitrary"),
        name="swa",
    )(sinks, big, small, small, small, small)


def _s5_kernel(u_ref, wb_ref, wc_ref, d_ref, sc_ref, o_ref, xr_ref, xi_ref, cr_ref, ci_ref):
    @pl.when(pl.program_id(1) == 0)
    def _():
        cr_ref[...] = jnp.zeros_like(cr_ref)
        ci_ref[...] = jnp.zeros_like(ci_ref)

    u = u_ref[...]
    bu = _dot(u.astype(BF16), wb_ref[...])
    xr_ref[...] = bu[:, :S5_LANES]
    xi_ref[...] = bu[:, S5_LANES:]

    def cmul_add(xr, xi, ar, ai, sr, si):
        return xr + (ar * sr - ai * si), xi + (ar * si + ai * sr)

    for w in range(S5_LANES // S5_SCAN_WIDTH):
        sl = slice(w * S5_SCAN_WIDTH, (w + 1) * S5_SCAN_WIDTH)
        consts = [sc_ref[c, :, sl] for c in range(8)]
        a1r, a1i, a2r, a2i, a4r, a4i, pr, pi = consts

        def body(t, carry):
            cr, ci = carry
            r0 = pl.multiple_of(t * SUBLANES, SUBLANES)
            xr = xr_ref[pl.ds(r0, SUBLANES), sl]
            xi = xi_ref[pl.ds(r0, SUBLANES), sl]
            xr, xi = cmul_add(xr, xi, a1r, a1i, pltpu.roll(xr, 1, 0), pltpu.roll(xi, 1, 0))
            xr, xi = cmul_add(xr, xi, a2r, a2i, pltpu.roll(xr, 2, 0), pltpu.roll(xi, 2, 0))
            xr, xi = cmul_add(xr, xi, a4r, a4i, pltpu.roll(xr, 4, 0), pltpu.roll(xi, 4, 0))
            xr, xi = cmul_add(xr, xi, pr, pi, cr, ci)
            xr_ref[pl.ds(r0, SUBLANES), sl] = xr
            xi_ref[pl.ds(r0, SUBLANES), sl] = xi
            return xr[SUBLANES - 1:SUBLANES, :], xi[SUBLANES - 1:SUBLANES, :]

        cr, ci = lax.fori_loop(0, S5_CHUNK // SUBLANES, body, (cr_ref[0:1, sl], ci_ref[0:1, sl]))
        cr_ref[0:1, sl] = cr
        ci_ref[0:1, sl] = ci

    xcat = jnp.concatenate([xr_ref[...].astype(BF16), xi_ref[...].astype(BF16)], axis=1)
    y = _dot(xcat, wc_ref[...]) + d_ref[...] * u
    o_ref[...] = jax.nn.gelu(y).astype(o_ref.dtype)


def _s5_prepare(lam_re, lam_im, log_dt, b_re, b_im, c_re, c_im):
    dt = jnp.exp(log_dt.astype(F32))[:, None]
    lr = lam_re.astype(F32)
    li = lam_im.astype(F32)
    mag = jnp.exp(lr * dt)
    ar = mag * jnp.cos(li * dt)
    ai = mag * jnp.sin(li * dt)
    den = lr * lr + li * li
    nr = ar - 1.0
    wr = (nr * lr + ai * li) / den
    wi = (ai * lr - nr * li) / den
    br_, bi_ = b_re.astype(F32), b_im.astype(F32)
    bbr = wr[..., None] * br_ - wi[..., None] * bi_
    bbi = wr[..., None] * bi_ + wi[..., None] * br_
    eye = jnp.eye(S5_GROUPS, dtype=F32)
    wbr = jnp.einsum('gpi,gh->gihp', bbr, eye).reshape(S5_WIDTH, S5_LANES)
    wbi = jnp.einsum('gpi,gh->gihp', bbi, eye).reshape(S5_WIDTH, S5_LANES)
    wb = jnp.concatenate([wbr, wbi], axis=1).astype(BF16)
    wcr = jnp.einsum('gip,gh->gphi', c_re.astype(F32), eye).reshape(S5_LANES, S5_WIDTH)
    wci = jnp.einsum('gip,gh->gphi', c_im.astype(F32), eye).reshape(S5_LANES, S5_WIDTH)
    wc = jnp.concatenate([wcr, -wci], axis=0).astype(BF16)
    ar_f = ar.reshape(1, S5_LANES)
    ai_f = ai.reshape(1, S5_LANES)

    def cmul(xr, xi, yr, yi):
        return xr * yr - xi * yi, xr * yi + xi * yr

    pows_r, pows_i = [ar_f], [ai_f]
    for _ in range(SUBLANES - 1):
        nr_, ni_ = cmul(pows_r[-1], pows_i[-1], ar_f, ai_f)
        pows_r.append(nr_)
        pows_i.append(ni_)
    t_idx = jnp.arange(SUBLANES)[:, None]

    def masked(k):
        return (jnp.where(t_idx >= k, pows_r[k - 1], 0.0), jnp.where(t_idx >= k, pows_i[k - 1], 0.0))

    a1 = masked(1)
    a2 = masked(2)
    a4 = masked(4)
    pr = jnp.concatenate(pows_r, axis=0)
    pi = jnp.concatenate(pows_i, axis=0)
    sc = jnp.stack([a1[0], a1[1], a2[0], a2[1], a4[0], a4[1], pr, pi], axis=0)
    return wb, wc, sc


def _s5(big, wb, wc, d, sc, *, batch, seq):
    n = batch * seq
    nc = seq // S5_CHUNK
    ucol = 5632 // S5_WIDTH
    return pl.pallas_call(
        _s5_kernel,
        grid=(batch, nc),
        in_specs=[
            pl.BlockSpec((S5_CHUNK, S5_WIDTH), lambda b, c: (b * nc + c, ucol)),
            pl.BlockSpec((S5_WIDTH, 2 * S5_LANES), lambda b, c: (0, 0)),
            pl.BlockSpec((2 * S5_LANES, S5_WIDTH), lambda b, c: (0, 0)),
            pl.BlockSpec((1, S5_WIDTH), lambda b, c: (0, 0)),
            pl.BlockSpec((8, SUBLANES, S5_LANES), lambda b, c: (0, 0, 0)),
        ],
        out_specs=pl.BlockSpec((S5_CHUNK, S5_WIDTH), lambda b, c: (b * nc + c, 0)),
        out_shape=jax.ShapeDtypeStruct((n, S5_WIDTH), BF16),
        scratch_shapes=[
            pltpu.VMEM((S5_CHUNK, S5_LANES), F32),
            pltpu.VMEM((S5_CHUNK, S5_LANES), F32),
            pltpu.VMEM((SUBLANES, S5_LANES), F32),
            pltpu.VMEM((SUBLANES, S5_LANES), F32),
        ],
        compiler_params=_params("parallel", "arbitrary"),
        name="s5",
    )(big, wb, wc, d.reshape(1, S5_WIDTH), sc)


def _ssd_kernel(z_ref, xs_ref, b_ref, c_ref, dt_ref, cw_ref, cb_ref, dtb_ref, alog_ref, dvec_ref, nw_ref,
                exp_h_ref, exp_l_ref, o_ref, buf_ref, st_ref):
    q = SSD_CHUNK

    @pl.when(pl.program_id(1) == 0)
    def _():
        buf_ref[0:SUBLANES, :] = jnp.zeros((SUBLANES, SSD_CONV_DIM), F32)
        st_ref[...] = jnp.zeros_like(st_ref)

    buf_ref[SUBLANES:SUBLANES + q, 0:SSD_INNER] = xs_ref[...]
    buf_ref[SUBLANES:SUBLANES + q, SSD_INNER:SSD_INNER + SSD_BC] = b_ref[...]
    buf_ref[SUBLANES:SUBLANES + q, SSD_INNER + SSD_BC:SSD_CONV_DIM] = c_ref[...]
    acc = jnp.broadcast_to(cb_ref[...], (q, SSD_CONV_DIM))
    for k in range(SSD_CONV):
        acc = acc + cw_ref[k:k + 1, :] * buf_ref[pl.ds(SUBLANES - (SSD_CONV - 1) + k, q), :]
    buf_ref[0:SUBLANES, :] = buf_ref[q:q + SUBLANES, :]
    xbc = acc * jax.nn.sigmoid(acc)
    xs = xbc[:, :SSD_INNER]
    bm = xbc[:, SSD_INNER:SSD_INNER + SSD_BC]
    cm = xbc[:, SSD_INNER + SSD_BC:]

    lane = lax.broadcasted_iota(jnp.int32, (q, LANES), 1)
    row = lax.broadcasted_iota(jnp.int32, (q, LANES), 0)
    head_lane = lane < SSD_HEADS
    dt_in = dt_ref[...] + dtb_ref[...]
    dt = jnp.maximum(dt_in, 0.0) + jnp.log1p(jnp.exp(-jnp.abs(dt_in)))
    dt = jnp.where(head_lane, dt, 0.0)
    a = -jnp.exp(alog_ref[...])
    ad = dt * a
    tri = (row >= lane).astype(BF16)
    cs = _exact_dot_rhs(tri, ad)
    cs_t = cs.T
    dt_t = dt.T
    total = cs[q - 1:q, :]
    dec_t = jnp.exp(cs_t[:, q - 1:q] - cs_t)
    w_t = dec_t * dt_t

    exp_h = exp_h_ref[...]
    exp_l = exp_l_ref[...]
    ecs_x = _exact_dot_lhs(jnp.exp(cs), exp_h)
    w_x = _exact_dot_lhs(w_t.T, exp_h)
    cs_cols = _exact_dot_lhs(cs, exp_l)
    tot_x = _exact_dot_lhs(jnp.broadcast_to(jnp.exp(total), (SUBLANES, LANES)), exp_h)[0:1, :]

    bm16 = bm.astype(BF16)
    cm16 = cm.astype(BF16)
    xs16 = xs.astype(BF16)
    low = lane < SSD_STATE
    g_mats = [_dot_nt(jnp.where(low, cm, 0.0).astype(BF16), bm16),
              _dot_nt(jnp.where(low, 0.0, cm).astype(BF16), bm16)]
    causal = row >= lane
    ydiag = []
    for j in range(SSD_HEADS // 2):
        slab = xs[:, j * LANES:(j + 1) * LANES]
        halves = (jnp.where(low, slab, 0.0).astype(BF16), jnp.where(low, 0.0, slab).astype(BF16))
        acc_j = None
        for half in range(2):
            h = 2 * j + half
            g = h // (SSD_HEADS // SSD_GROUPS)
            seg = cs_cols[:, h * LANES:(h + 1) * LANES] - cs_t[h:h + 1, :]
            lmat = jnp.exp(jnp.where(causal, seg, NEG_INF))
            m = (g_mats[g] * lmat * dt_t[h:h + 1, :]).astype(BF16)
            part = _dot(m, halves[half])
            acc_j = part if acc_j is None else acc_j + part
        ydiag.append(acc_j)
    y = jnp.concatenate(ydiag, axis=1)

    st = st_ref[...]
    y = y + _dot(cm16, st.astype(BF16)) * ecs_x
    new = _dot(bm.T.astype(BF16), (xs * w_x).astype(BF16))
    srow = lax.broadcasted_iota(jnp.int32, (LANES, SSD_INNER), 0)
    scol = lax.broadcasted_iota(jnp.int32, (LANES, SSD_INNER), 1)
    same_group = (srow < SSD_STATE) == (scol < SSD_INNER // SSD_GROUPS)
    st_ref[...] = jnp.where(same_group, st * tot_x + new, 0.0)

    y = y + dvec_ref[...] * xs
    zz = z_ref[...]
    gated = y * (zz * jax.nn.sigmoid(zz))
    out = gated * lax.rsqrt(jnp.mean(gated * gated, axis=-1, keepdims=True) + EPS) * nw_ref[...]
    o_ref[...] = out.astype(o_ref.dtype)


def _ssd(big, small, conv_w, conv_b, dt_bias, a_log, dvec, norm_w, *, batch, seq):
    n = batch * seq
    nc = seq // SSD_CHUNK
    q = SSD_CHUNK
    pad = LANES - SSD_HEADS
    dtb = jnp.pad(dt_bias.astype(F32), (0, pad)).reshape(1, LANES)
    alog = jnp.pad(a_log.astype(F32), (0, pad)).reshape(1, LANES)
    d_x = jnp.repeat(dvec.astype(F32), SSD_HEAD_DIM).reshape(1, SSD_INNER)
    heads = np.arange(LANES)[:, None]
    exp_h = jnp.asarray(heads == (np.arange(SSD_INNER)[None, :] // SSD_HEAD_DIM), dtype=BF16)
    exp_l = jnp.asarray(heads == (np.arange(SSD_HEADS * LANES)[None, :] // LANES), dtype=BF16)

    def blk(col, width):
        return pl.BlockSpec((q, width), lambda b, c: (b * nc + c, col))

    def full(shape):
        return pl.BlockSpec(shape, lambda b, c: (0,) * len(shape))

    return pl.pallas_call(
        _ssd_kernel,
        grid=(batch, nc),
        in_specs=[
            blk(0, SSD_INNER), blk(1, SSD_INNER),
            blk(2, LANES), blk(3, LANES), blk(4, LANES),
            full((SSD_CONV, SSD_CONV_DIM)), full((1, SSD_CONV_DIM)),
            full((1, LANES)), full((1, LANES)), full((1, SSD_INNER)), full((1, SSD_INNER)),
            full((LANES, SSD_INNER)), full((LANES, SSD_HEADS * LANES)),
        ],
        out_specs=pl.BlockSpec((q, SSD_INNER), lambda b, c: (b * nc + c, 0)),
        out_shape=jax.ShapeDtypeStruct((n, SSD_INNER), BF16),
        scratch_shapes=[
            pltpu.VMEM((q + SUBLANES, SSD_CONV_DIM), F32),
            pltpu.VMEM((LANES, SSD_INNER), F32),
        ],
        compiler_params=_params("parallel", "arbitrary"),
        name="ssd",
    )(big, big, small, small, small, conv_w.astype(F32), conv_b.astype(F32).reshape(1, SSD_CONV_DIM),
      dtb, alog, d_x, norm_w.astype(F32).reshape(1, SSD_INNER), exp_h, exp_l)


def _merge_kernel(attn_ref, ys5_ref, ssd_ref, ga_ref, gb_ref, gc_ref, x_ref,
                  wa_ref, wga_ref, wgb_ref, wc_ref, wo_ref, o_ref):
    ys5 = ys5_ref[...]
    br_a = _dot(attn_ref[...], wa_ref[...])
    br_b = _dot(ys5, wga_ref[...]) * jax.nn.sigmoid(_dot(ys5, wgb_ref[...]))
    br_c = _dot(ssd_ref[...], wc_ref[...])
    merged = (jax.nn.sigmoid(ga_ref[...]) * br_a + jax.nn.sigmoid(gb_ref[...]) * br_b
              + jax.nn.sigmoid(gc_ref[...]) * br_c)
    o_ref[...] = x_ref[...] + _dot(merged.astype(BF16), wo_ref[...])


def _merge(attn, ys5, ssd, big, x, wa, wga, wgb, wc, wo, *, tm=256):
    n = x.shape[0]

    def rows(width, col=0):
        return pl.BlockSpec((tm, width), lambda i: (i, col))

    def full(a):
        return pl.BlockSpec(a.shape, lambda i: (0, 0))

    return pl.pallas_call(
        _merge_kernel,
        grid=(n // tm,),
        in_specs=[rows(ATTN_WIDTH), rows(S5_WIDTH), rows(SSD_INNER),
                  rows(D_MODEL, 2), rows(D_MODEL, 3), rows(D_MODEL, 4), rows(D_MODEL),
                  full(wa), full(wga), full(wgb), full(wc), full(wo)],
        out_specs=rows(D_MODEL),
        out_shape=jax.ShapeDtypeStruct((n, D_MODEL), F32),
        compiler_params=_params("parallel"),
        name="merge",
    )(attn, ys5, ssd, big, big, big, x, wa, wga, wgb, wc, wo)


def _xattn_kernel(x_ref, g_ref, kv_ref, wq_ref, wo_ref, o_ref):
    x = x_ref[...]
    h = (x * lax.rsqrt(jnp.mean(x * x, axis=-1, keepdims=True) + EPS) * g_ref[...]).astype(BF16)
    q = (_dot(h, wq_ref[...]) * (XA_HEAD_DIM ** -0.5)).astype(BF16)
    kv = kv_ref[...]
    k = kv[:, :XA_WIDTH]
    v = kv[:, XA_WIDTH:]
    lane = lax.broadcasted_iota(jnp.int32, k.shape, 1)
    o = None
    for hd in range(XA_HEADS):
        sel = (lane >= hd * XA_HEAD_DIM) & (lane < (hd + 1) * XA_HEAD_DIM)
        kh = jnp.where(sel, k, 0.0).astype(BF16)
        vh = jnp.where(sel, v, 0.0).astype(BF16)
        s = _dot_nt(q, kh)
        s = s - jnp.max(s, axis=-1, keepdims=True)
        p = jnp.exp(s)
        p = p / jnp.sum(p, axis=-1, keepdims=True)
        part = _dot(p.astype(BF16), vh)
        o = part if o is None else o + part
    o_ref[...] = x + _dot(o.astype(BF16), wo_ref[...])


def _xattn(x, g, kv, wq, wo, *, batch, seq, n_mem, tm=256):
    n = batch * seq
    nt = seq // tm
    return pl.pallas_call(
        _xattn_kernel,
        grid=(batch, nt),
        in_specs=[
            pl.BlockSpec((tm, D_MODEL), lambda b, i: (b * nt + i, 0)),
            pl.BlockSpec((1, D_MODEL), lambda b, i: (0, 0)),
            pl.BlockSpec((n_mem, 2 * XA_WIDTH), lambda b, i: (b, 0)),
            pl.BlockSpec((D_MODEL, XA_WIDTH), lambda b, i: (0, 0)),
            pl.BlockSpec((XA_WIDTH, D_MODEL), lambda b, i: (0, 0)),
        ],
        out_specs=pl.BlockSpec((tm, D_MODEL), lambda b, i: (b * nt + i, 0)),
        out_shape=jax.ShapeDtypeStruct((n, D_MODEL), F32),
        compiler_params=_params("parallel", "parallel"),
        name="xattn",
    )(x, g.reshape(1, D_MODEL), kv, wq, wo)


_PEER_CELLS = [(i, j) for i in range(PEER_TOPK) for j in range(PEER_TOPK) if (i + 1) * (j + 1) <= PEER_TOPK]


def _peer_route_kernel(q_ref, k1_ref, k2_ref, e1_ref, e2_ref, thr_ref, rows_ref, kept_ref):
    tm = q_ref.shape[0]
    keys = (k1_ref[...], k2_ref[...])
    kidx = lax.broadcasted_iota(jnp.int32, (PEER_N_KEYS, tm), 0).astype(F32)

    def store(which, hd, st, rows, vals):
        for r in range(PEER_TOPK):
            rows_ref[which, r, hd:hd + 1, :] = rows[r] - rows[0]
        kept_ref[which, hd] = jnp.where(vals == NEG_INF, st - rows[0], NEG_INF)

    def scores(hd, which):
        c0 = hd * PEER_KEY_DIM + which * PEER_HALF
        return _dot_nt(keys[which], q_ref[:, c0:c0 + PEER_HALF])

    tied = jnp.zeros((1, tm), F32)
    for hd in range(PEER_HEADS):
        for which in range(2):
            st = scores(hd, which)
            vals = st
            rows = []
            for _ in range(PEER_TOPK):
                m = jnp.max(vals, axis=0, keepdims=True)
                vals = jnp.where(vals == m, NEG_INF, vals)
                rows.append(m)
            store(which, hd, st, rows, vals)
            removed = jnp.sum(jnp.where(vals == NEG_INF, 1.0, 0.0), axis=0, keepdims=True)
            tied = jnp.maximum(tied, jnp.abs(removed - float(PEER_TOPK)))

    @pl.when(jnp.max(tied) > 0.0)
    def _():
        for hd in range(PEER_HEADS):
            for which in range(2):
                st = scores(hd, which)
                vals = st
                rows = []
                for _ in range(PEER_TOPK):
                    m = jnp.max(vals, axis=0, keepdims=True)
                    first = jnp.min(jnp.where(vals == m, kidx, float(PEER_N_KEYS)), axis=0, keepdims=True)
                    vals = jnp.where(kidx == first, NEG_INF, vals)
                    rows.append(m)
                store(which, hd, st, rows, vals)

    v1 = [rows_ref[0, r] for r in range(PEER_TOPK)]
    v2 = [rows_ref[1, r] for r in range(PEER_TOPK)]
    cands = [v1[i] + v2[j] for (i, j) in _PEER_CELLS]
    work = cands
    cum = jnp.zeros_like(v1[0])
    tau = jnp.full_like(v1[0], NEG_INF)
    zsum = jnp.zeros_like(v1[0])
    for _ in range(PEER_TOPK):
        m = functools.reduce(jnp.maximum, work)
        cnt = jnp.zeros_like(m)
        nxt = []
        for c in work:
            eq = c == m
            cnt = cnt + jnp.where(eq, 1.0, 0.0)
            nxt.append(jnp.where(eq, NEG_INF, c))
        work = nxt
        open_ = cum < PEER_TOPK
        used = jnp.minimum(cnt, PEER_TOPK - cum)
        zsum = zsum + jnp.where(open_, used * jnp.exp(m), 0.0)
        tau = jnp.where(open_, m, tau)
        cum = cum + cnt
    log_z = jnp.log(zsum)
    e1v = [jnp.exp(v - log_z) for v in v1]
    e2v = [jnp.exp(v) for v in v2]
    thr = jnp.full_like(tau, jnp.inf)
    for (i, j), c in zip(_PEER_CELLS, cands):
        thr = jnp.minimum(thr, jnp.where(c >= tau, e1v[i] * e2v[j], jnp.inf))
    thr_ref[0] = thr
    for hd in range(PEER_HEADS):
        e1_ref[hd, 0] = jnp.exp(kept_ref[0, hd] - log_z[hd:hd + 1, :])
        e2_ref[hd, 0] = jnp.exp(kept_ref[1, hd])


def _peer_route(q, k1, k2):
    n = q.shape[0]
    tm = LANES
    ns = n // tm
    big_shape = jax.ShapeDtypeStruct((PEER_HEADS, ns, PEER_N_KEYS, tm), F32)
    big_spec = pl.BlockSpec((PEER_HEADS, 1, PEER_N_KEYS, tm), lambda i: (0, i, 0, 0))
    return pl.pallas_call(
        _peer_route_kernel,
        grid=(ns,),
        in_specs=[
            pl.BlockSpec((tm, PEER_HEADS * PEER_KEY_DIM), lambda i: (i, 0)),
            pl.BlockSpec((PEER_N_KEYS, PEER_HALF), lambda i: (0, 0)),
            pl.BlockSpec((PEER_N_KEYS, PEER_HALF), lambda i: (0, 0)),
        ],
        out_specs=[big_spec, big_spec, pl.BlockSpec((1, PEER_HEADS, tm), lambda i: (i, 0, 0))],
        out_shape=[big_shape, big_shape, jax.ShapeDtypeStruct((ns, PEER_HEADS, tm), F32)],
        scratch_shapes=[
            pltpu.VMEM((2, PEER_TOPK, PEER_HEADS, tm), F32),
            pltpu.VMEM((2, PEER_HEADS, PEER_N_KEYS, tm), F32),
        ],
        compiler_params=_params("parallel"),
        name="peer_route",
    )(q, k1, k2)


PEER_TE = 512
PEER_TM = 1024
PEER_STRIP = 256


def _peer_dense_kernel(x_ref, h_ref, e1_ref, e2_ref, thr_ref, u_ref, vt_ref, o_ref, acc_ref, act_ref, p_ref):
    j = pl.program_id(1)
    n_slab = PEER_TE // PEER_N_KEYS

    @pl.when(j == 0)
    def _():
        acc_ref[...] = jnp.zeros_like(acc_ref)

    u = u_ref[...]
    vt = vt_ref[...]
    wide = PEER_STRIP
    n_strip = PEER_TM // wide

    def first_matmul(s):
        r0 = pl.multiple_of(s * wide, wide)
        return _dot_nt(u, h_ref[pl.ds(r0, wide), :])

    act_ref[0] = first_matmul(0)
    p_ref[1] = jnp.zeros(p_ref.shape[1:], BF16)

    def one_strip(s, slot):
        act_ref[1 - slot] = first_matmul(jnp.minimum(s + 1, n_strip - 1))
        acc_ref[jnp.maximum(s - 1, 0)] += _dot(vt, p_ref[1 - slot])
        for half in range(wide // LANES):
            tc = s * (wide // LANES) + half
            thr = thr_ref[tc]
            for al in range(n_slab):
                a = j * n_slab + al
                w = None
                for hd in range(PEER_HEADS):
                    prod = e1_ref[hd, tc, pl.ds(a, 1), :] * e2_ref[hd, tc]
                    contrib = jnp.where(prod >= thr[hd:hd + 1, :], prod, 0.0)
                    w = contrib if w is None else w + contrib
                rows = slice(al * PEER_N_KEYS, (al + 1) * PEER_N_KEYS)
                lanes = slice(half * LANES, (half + 1) * LANES)
                g = jax.nn.gelu(act_ref[slot, rows, lanes])
                p_ref[slot, rows, lanes] = (w * g).astype(BF16)

    def pair(k, carry):
        one_strip(2 * k, 0)
        one_strip(2 * k + 1, 1)
        return carry

    lax.fori_loop(0, n_strip // 2, pair, 0)
    acc_ref[n_strip - 1] += _dot(vt, p_ref[(n_strip - 1) % 2])

    @pl.when(j == pl.num_programs(1) - 1)
    def _():
        for s in range(n_strip):
            o_ref[s * wide:(s + 1) * wide, :] = x_ref[s * wide:(s + 1) * wide, :] + acc_ref[s].T


def _peer_dense(x, h, s1, s2, tau, u, vt):
    n = x.shape[0]
    tm, te = PEER_TM, PEER_TE
    ns = tm // LANES
    big_spec = pl.BlockSpec((PEER_HEADS, ns, PEER_N_KEYS, LANES), lambda i, j: (0, i, 0, 0))
    return pl.pallas_call(
        _peer_dense_kernel,
        grid=(n // tm, PEER_N_EXPERTS // te),
        in_specs=[
            pl.BlockSpec((tm, D_MODEL), lambda i, j: (i, 0)),
            pl.BlockSpec((tm, D_MODEL), lambda i, j: (i, 0)),
            big_spec, big_spec,
            pl.BlockSpec((ns, PEER_HEADS, LANES), lambda i, j: (i, 0, 0)),
            pl.BlockSpec((te, D_MODEL), lambda i, j: (j, 0)),
            pl.BlockSpec((D_MODEL, te), lambda i, j: (0, j)),
        ],
        out_specs=pl.BlockSpec((tm, D_MODEL), lambda i, j: (i, 0)),
        out_shape=jax.ShapeDtypeStruct((n, D_MODEL), F32),
        scratch_shapes=[
            pltpu.VMEM((tm // PEER_STRIP, D_MODEL, PEER_STRIP), F32),
            pltpu.VMEM((2, te, PEER_STRIP), F32),
            pltpu.VMEM((2, te, PEER_STRIP), BF16),
        ],
        compiler_params=_params("parallel", "arbitrary"),
        name="peer_dense",
    )(x, h, s1, s2, tau, u, vt)


def _final_norm_kernel(x_ref, g_ref, o_ref):
    x = x_ref[...]
    o_ref[...] = x * lax.rsqrt(jnp.mean(x * x, axis=-1, keepdims=True) + EPS) * g_ref[...]


def _final_norm(x, g, *, tm=512):
    n, d = x.shape
    return pl.pallas_call(
        _final_norm_kernel,
        grid=(n // tm,),
        in_specs=[pl.BlockSpec((tm, d), lambda i: (i, 0)), pl.BlockSpec((1, d), lambda i: (0, 0))],
        out_specs=pl.BlockSpec((tm, d), lambda i: (i, 0)),
        out_shape=jax.ShapeDtypeStruct((n, d), F32),
        compiler_params=_params("parallel"),
        name="final_norm",
    )(x, g.reshape(1, d))


def _q_head_perm():
    cols = []
    for j in range(ATTN_HEADS // 2):
        for hd in (j, j + ATTN_HEADS // 2):
            cols.extend(range(hd * ATTN_HEAD_DIM, (hd + 1) * ATTN_HEAD_DIM))
    return np.asarray(cols, dtype=np.int32)


def _split_w_in(w):
    o = 0
    q = w[:, o:o + 512]; o += 512
    k = w[:, o:o + 128]; o += 128
    v = w[:, o:o + 128]; o += 128
    u = w[:, o:o + 512]; o += 512
    z = w[:, o:o + 1024]; o += 1024
    xs = w[:, o:o + 1024]; o += 1024
    bm = w[:, o:o + 128]; o += 128
    cm = w[:, o:o + 128]; o += 128
    dt = w[:, o:o + 16]; o += 16
    gates = w[:, o:o + 3072]
    q = q[:, _q_head_perm()]
    big = jnp.concatenate([z, xs, gates, q, u], axis=1).astype(BF16)
    small = jnp.concatenate([k, v, bm, cm, dt, jnp.zeros((w.shape[0], LANES - SSD_HEADS), w.dtype)], axis=1)
    return big, small.astype(BF16)


def kernel(x, mem, norm_mix, w_in, attn_sinks, s5_lambda_re, s5_lambda_im, s5_log_dt, s5_b_re, s5_b_im, s5_c_re, s5_c_im, s5_d, s5_glu_a, s5_glu_b, ssd_conv_w, ssd_conv_b, ssd_dt_bias, ssd_a_log, ssd_d, ssd_norm, w_attn_out, w_ssd_out, w_o, norm_xattn, norm_mem, xa_wq, xa_wkv, xa_wo, norm_ffn, peer_wq, peer_k1, peer_k2, peer_u, peer_v, norm_final):
    batch, seq, d = x.shape
    n = batch * seq
    n_mem = mem.shape[1]
    depth = w_in.shape[0]
    xf = x.reshape(n, d).astype(F32)
    memf = mem.reshape(batch * n_mem, d).astype(F32)
    for i in range(depth):
        w_big, w_small = _split_w_in(w_in[i])
        big = _norm_mm(xf, norm_mix[i], w_big, tm=1024, tn=512, out_dtype=F32, name="in_proj_big")
        small = _norm_mm(xf, norm_mix[i], w_small, tm=1024, tn=SMALL_COLS, out_dtype=F32, name="in_proj_small")
        attn = _swa(big, small, attn_sinks[i].astype(F32), batch=batch, seq=seq)
        wb, wc, sc = _s5_prepare(s5_lambda_re[i], s5_lambda_im[i], s5_log_dt[i], s5_b_re[i], s5_b_im[i],
                                 s5_c_re[i], s5_c_im[i])
        ys5 = _s5(big, wb, wc, s5_d[i].astype(F32), sc, batch=batch, seq=seq)
        ssd = _ssd(big, small, ssd_conv_w[i], ssd_conv_b[i], ssd_dt_bias[i], ssd_a_log[i], ssd_d[i], ssd_norm[i],
                   batch=batch, seq=seq)
        xf = _merge(attn, ys5, ssd, big, xf,
                    w_attn_out[i][_q_head_perm(), :].astype(BF16), s5_glu_a[i].astype(BF16),
                    s5_glu_b[i].astype(BF16), w_ssd_out[i].astype(BF16), w_o[i].astype(BF16))
        kv = _norm_mm(memf, norm_mem[i], xa_wkv[i].astype(BF16), tm=batch * n_mem, tn=2 * XA_WIDTH,
                      out_dtype=F32, name="mem_kv")
        xf = _xattn(xf, norm_xattn[i], kv, xa_wq[i].astype(BF16), xa_wo[i].astype(BF16),
                    batch=batch, seq=seq, n_mem=n_mem)
        q, h = _norm_mm(xf, norm_ffn[i], peer_wq[i].astype(BF16), tm=1024, tn=512, out_dtype=BF16,
                        with_h=True, name="peer_query")
        s1, s2, tau = _peer_route(q, peer_k1[i].astype(BF16), peer_k2[i].astype(BF16))
        xf = _peer_dense(xf, h, s1, s2, tau, peer_u[i].astype(BF16), peer_v[i].astype(BF16).T)
    return _final_norm(xf, norm_final.astype(F32)).reshape(batch, seq, d)
```

```python
import functools
import math

import jax
import jax.numpy as jnp
import numpy as np
from jax import lax
from jax.experimental import pallas as pl
from jax.experimental.pallas import tpu as pltpu

F32 = jnp.float32
BF16 = jnp.bfloat16
NEG_INF = float("-inf")

D_MODEL = 1024
EPS = 1e-6
ATTN_HEADS = 8
ATTN_HEAD_DIM = 64
ATTN_WIDTH = ATTN_HEADS * ATTN_HEAD_DIM
ATTN_BLOCK = 128
S5_WIDTH = 512
S5_GROUP = 16
S5_GROUPS = S5_WIDTH // S5_GROUP
S5_STATE = 64
S5_LANES = S5_GROUPS * S5_STATE
S5_CHUNK = 256
S5_SCAN_WIDTH = 512
SSD_INNER = 1024
SSD_HEAD_DIM = 64
SSD_HEADS = SSD_INNER // SSD_HEAD_DIM
SSD_STATE = 64
SSD_GROUPS = 2
SSD_CONV = 4
SSD_CHUNK = 128
SSD_BC = SSD_GROUPS * SSD_STATE
SSD_CONV_DIM = SSD_INNER + 2 * SSD_BC
XA_HEADS = 4
XA_HEAD_DIM = 64
XA_WIDTH = XA_HEADS * XA_HEAD_DIM
PEER_HEADS = 8
PEER_KEY_DIM = 256
PEER_HALF = PEER_KEY_DIM // 2
PEER_N_KEYS = 128
PEER_TOPK = 16
PEER_N_EXPERTS = PEER_N_KEYS * PEER_N_KEYS
LANES = 128
SUBLANES = 8
VMEM_LIMIT_BYTES = 56 * 1024 * 1024

BIG_COLS = 6144
SMALL_COLS = 640


def _params(*sem):
    return pltpu.CompilerParams(dimension_semantics=sem, vmem_limit_bytes=VMEM_LIMIT_BYTES)


def _dot(a, b):
    return jnp.dot(a, b, preferred_element_type=F32)


def _dot_nt(a, b):
    return lax.dot_general(a, b, (((1,), (1,)), ((), ())), preferred_element_type=F32)


def _split3(x):
    p1 = x.astype(BF16)
    r1 = x - p1.astype(F32)
    p2 = r1.astype(BF16)
    p3 = (r1 - p2.astype(F32)).astype(BF16)
    return p1, p2, p3


def _exact_dot_rhs(sel, x):
    p1, p2, p3 = _split3(x)
    return _dot(sel, p1) + _dot(sel, p2) + _dot(sel, p3)


def _exact_dot_lhs(x, sel):
    p1, p2, p3 = _split3(x)
    return _dot(p1, sel) + _dot(p2, sel) + _dot(p3, sel)


def _norm_mm_kernel(x_ref, g_ref, w_ref, o_ref, *rest, with_h):
    if with_h:
        hout_ref, h_ref = rest
    else:
        (h_ref,) = rest

    @pl.when(pl.program_id(1) == 0)
    def _():
        x = x_ref[...]
        h = x * lax.rsqrt(jnp.mean(x * x, axis=-1, keepdims=True) + EPS) * g_ref[...]
        h_ref[...] = h.astype(BF16)
        if with_h:
            for s in range(hout_ref.shape[0]):
                hout_ref[s] = h[s * PEER_STRIP:(s + 1) * PEER_STRIP, :].T.astype(BF16)

    o_ref[...] = _dot(h_ref[...], w_ref[...]).astype(o_ref.dtype)


def _norm_mm(x, g, w, *, tm, tn, out_dtype, with_h=False, name):
    n, k = x.shape
    m = w.shape[1]
    assert n % tm == 0 and m % tn == 0
    out_shape = [jax.ShapeDtypeStruct((n, m), out_dtype)]
    out_specs = [pl.BlockSpec((tm, tn), lambda i, j: (i, j))]
    if with_h:
        out_shape.append(jax.ShapeDtypeStruct((n // PEER_STRIP, k, PEER_STRIP), BF16))
        out_specs.append(pl.BlockSpec((tm // PEER_STRIP, k, PEER_STRIP), lambda i, j: (i, 0, 0)))
    res = pl.pallas_call(
        functools.partial(_norm_mm_kernel, with_h=with_h),
        grid=(n // tm, m // tn),
        in_specs=[
            pl.BlockSpec((tm, k), lambda i, j: (i, 0)),
            pl.BlockSpec((1, k), lambda i, j: (0, 0)),
            pl.BlockSpec((k, tn), lambda i, j: (0, j)),
        ],
        out_specs=out_specs,
        out_shape=out_shape,
        scratch_shapes=[pltpu.VMEM((tm, k), BF16)],
        compiler_params=_params("parallel", "arbitrary"),
        name=name,
    )(x, g.reshape(1, k), w)
    return res if with_h else res[0]


def _swa_kernel(sink_ref, q_ref, kc_ref, kp_ref, vc_ref, vp_ref, o_ref):
    nblk = pl.program_id(1)
    blk = ATTN_BLOCK
    q = q_ref[...]
    k = jnp.concatenate([kp_ref[...], kc_ref[...]], axis=0)
    v = jnp.concatenate([vp_ref[...], vc_ref[...]], axis=0).astype(BF16)
    lane_k = lax.broadcasted_iota(jnp.int32, (2 * blk, LANES), 1)
    k_lo = jnp.where(lane_k < ATTN_HEAD_DIM, k, 0.0).astype(BF16)
    k_hi = jnp.where(lane_k < ATTN_HEAD_DIM, 0.0, k).astype(BF16)
    qi = lax.broadcasted_iota(jnp.int32, (blk, 2 * blk), 0)
    kj = lax.broadcasted_iota(jnp.int32, (blk, 2 * blk), 1)
    dist = qi - kj + blk
    valid = (dist >= 0) & (dist < blk) & ((nblk > 0) | (kj >= blk))
    distf = dist.astype(F32)
    lane_o = lax.broadcasted_iota(jnp.int32, (blk, LANES), 1)
    outs = []
    for j in range(ATTN_HEADS // 2):
        qs = (q[:, j * LANES:(j + 1) * LANES] * (ATTN_HEAD_DIM ** -0.5)).astype(BF16)
        res = []
        for half, kk in ((0, k_lo), (1, k_hi)):
            head = j + 4 * half
            slope = 2.0 ** (-8.0 * (head + 1) / ATTN_HEADS)
            s = _dot_nt(qs, kk) - slope * distf
            s = jnp.where(valid, s, NEG_INF)
            sink = sink_ref[head]
            m = jnp.maximum(jnp.max(s, axis=-1, keepdims=True), sink)
            p = jnp.exp(s - m)
            denom = jnp.sum(p, axis=-1, keepdims=True) + jnp.exp(sink - m)
            p = p / denom
            res.append(_dot(p.astype(BF16), v))
        outs.append(jnp.where(lane_o < ATTN_HEAD_DIM, res[0], res[1]))
    o_ref[...] = jnp.concatenate(outs, axis=1).astype(o_ref.dtype)


def _swa(big, small, sinks, *, batch, seq):
    nb = seq // ATTN_BLOCK
    n = batch * seq
    qcol = 5120 // ATTN_WIDTH

    def cur(col):
        return lambda b, i: (b * nb + i, col)

    def prev(col):
        return lambda b, i: (b * nb + jnp.maximum(i - 1, 0), col)

    return pl.pallas_call(
        _swa_kernel,
        grid=(batch, nb),
        in_specs=[
            pl.BlockSpec(memory_space=pltpu.SMEM),
            pl.BlockSpec((ATTN_BLOCK, ATTN_WIDTH), cur(qcol)),
            pl.BlockSpec((ATTN_BLOCK, LANES), cur(0)),
            pl.BlockSpec((ATTN_BLOCK, LANES), prev(0)),
            pl.BlockSpec((ATTN_BLOCK, LANES), cur(1)),
            pl.BlockSpec((ATTN_BLOCK, LANES), prev(1)),
        ],
        out_specs=pl.BlockSpec((ATTN_BLOCK, ATTN_WIDTH), lambda b, i: (b * nb + i, 0)),
        out_shape=jax.ShapeDtypeStruct((n, ATTN_WIDTH), BF16),
        compiler_params=_params("parallel", "arbitrary"),
        name="swa",
    )(sinks, big, small, small, small, small)


def _s5_kernel(u_ref, wb_ref, wc_ref, d_ref, sc_ref, o_ref, xr_ref, xi_ref, cr_ref, ci_ref):
    @pl.when(pl.program_id(1) == 0)
    def _():
        cr_ref[...] = jnp.zeros_like(cr_ref)
        ci_ref[...] = jnp.zeros_like(ci_ref)

    u = u_ref[...]
    bu = _dot(u.astype(BF16), wb_ref[...])
    xr_ref[...] = bu[:, :S5_LANES]
    xi_ref[...] = bu[:, S5_LANES:]

    def cmul_add(xr, xi, ar, ai, sr, si):
        return xr + (ar * sr - ai * si), xi + (ar * si + ai * sr)

    for w in range(S5_LANES // S5_SCAN_WIDTH):
        sl = slice(w * S5_SCAN_WIDTH, (w + 1) * S5_SCAN_WIDTH)
        consts = [sc_ref[c, :, sl] for c in range(8)]
        a1r, a1i, a2r, a2i, a4r, a4i, pr, pi = consts

        def body(t, carry):
            cr, ci = carry
            r0 = pl.multiple_of(t * SUBLANES, SUBLANES)
            xr = xr_ref[pl.ds(r0, SUBLANES), sl]
            xi = xi_ref[pl.ds(r0, SUBLANES), sl]
            xr, xi = cmul_add(xr, xi, a1r, a1i, pltpu.roll(xr, 1, 0), pltpu.roll(xi, 1, 0))
            xr, xi = cmul_add(xr, xi, a2r, a2i, pltpu.roll(xr, 2, 0), pltpu.roll(xi, 2, 0))
            xr, xi = cmul_add(xr, xi, a4r, a4i, pltpu.roll(xr, 4, 0), pltpu.roll(xi, 4, 0))
            xr, xi = cmul_add(xr, xi, pr, pi, cr, ci)
            xr_ref[pl.ds(r0, SUBLANES), sl] = xr
            xi_ref[pl.ds(r0, SUBLANES), sl] = xi
            return xr[SUBLANES - 1:SUBLANES, :], xi[SUBLANES - 1:SUBLANES, :]

        cr, ci = lax.fori_loop(0, S5_CHUNK // SUBLANES, body, (cr_ref[0:1, sl], ci_ref[0:1, sl]))
        cr_ref[0:1, sl] = cr
        ci_ref[0:1, sl] = ci

    xcat = jnp.concatenate([xr_ref[...].astype(BF16), xi_ref[...].astype(BF16)], axis=1)
    y = _dot(xcat, wc_ref[...]) + d_ref[...] * u
    o_ref[...] = jax.nn.gelu(y).astype(o_ref.dtype)


def _s5_prepare(lam_re, lam_im, log_dt, b_re, b_im, c_re, c_im):
    dt = jnp.exp(log_dt.astype(F32))[:, None]
    lr = lam_re.astype(F32)
    li = lam_im.astype(F32)
    mag = jnp.exp(lr * dt)
    ar = mag * jnp.cos(li * dt)
    ai = mag * jnp.sin(li * dt)
    den = lr * lr + li * li
    nr = ar - 1.0
    wr = (nr * lr + ai * li) / den
    wi = (ai * lr - nr * li) / den
    br_, bi_ = b_re.astype(F32), b_im.astype(F32)
    bbr = wr[..., None] * br_ - wi[..., None] * bi_
    bbi = wr[..., None] * bi_ + wi[..., None] * br_
    eye = jnp.eye(S5_GROUPS, dtype=F32)
    wbr = jnp.einsum('gpi,gh->gihp', bbr, eye).reshape(S5_WIDTH, S5_LANES)
    wbi = jnp.einsum('gpi,gh->gihp', bbi, eye).reshape(S5_WIDTH, S5_LANES)
    wb = jnp.concatenate([wbr, wbi], axis=1).astype(BF16)
    wcr = jnp.einsum('gip,gh->gphi', c_re.astype(F32), eye).reshape(S5_LANES, S5_WIDTH)
    wci = jnp.einsum('gip,gh->gphi', c_im.astype(F32), eye).reshape(S5_LANES, S5_WIDTH)
    wc = jnp.concatenate([wcr, -wci], axis=0).astype(BF16)
    ar_f = ar.reshape(1, S5_LANES)
    ai_f = ai.reshape(1, S5_LANES)

    def cmul(xr, xi, yr, yi):
        return xr * yr - xi * yi, xr * yi + xi * yr

    pows_r, pows_i = [ar_f], [ai_f]
    for _ in range(SUBLANES - 1):
        nr_, ni_ = cmul(pows_r[-1], pows_i[-1], ar_f, ai_f)
        pows_r.append(nr_)
        pows_i.append(ni_)
    t_idx = jnp.arange(SUBLANES)[:, None]

    def masked(k):
        return (jnp.where(t_idx >= k, pows_r[k - 1], 0.0), jnp.where(t_idx >= k, pows_i[k - 1], 0.0))

    a1 = masked(1)
    a2 = masked(2)
    a4 = masked(4)
    pr = jnp.concatenate(pows_r, axis=0)
    pi = jnp.concatenate(pows_i, axis=0)
    sc = jnp.stack([a1[0], a1[1], a2[0], a2[1], a4[0], a4[1], pr, pi], axis=0)
    return wb, wc, sc


def _s5(big, wb, wc, d, sc, *, batch, seq):
    n = batch * seq
    nc = seq // S5_CHUNK
    ucol = 5632 // S5_WIDTH
    return pl.pallas_call(
        _s5_kernel,
        grid=(batch, nc),
        in_specs=[
            pl.BlockSpec((S5_CHUNK, S5_WIDTH), lambda b, c: (b * nc + c, ucol)),
            pl.BlockSpec((S5_WIDTH, 2 * S5_LANES), lambda b, c: (0, 0)),
            pl.BlockSpec((2 * S5_LANES, S5_WIDTH), lambda b, c: (0, 0)),
            pl.BlockSpec((1, S5_WIDTH), lambda b, c: (0, 0)),
            pl.BlockSpec((8, SUBLANES, S5_LANES), lambda b, c: (0, 0, 0)),
        ],
        out_specs=pl.BlockSpec((S5_CHUNK, S5_WIDTH), lambda b, c: (b * nc + c, 0)),
        out_shape=jax.ShapeDtypeStruct((n, S5_WIDTH), BF16),
        scratch_shapes=[
            pltpu.VMEM((S5_CHUNK, S5_LANES), F32),
            pltpu.VMEM((S5_CHUNK, S5_LANES), F32),
            pltpu.VMEM((SUBLANES, S5_LANES), F32),
            pltpu.VMEM((SUBLANES, S5_LANES), F32),
        ],
        compiler_params=_params("parallel", "arbitrary"),
        name="s5",
    )(big, wb, wc, d.reshape(1, S5_WIDTH), sc)


def _ssd_kernel(z_ref, xs_ref, b_ref, c_ref, dt_ref, cw_ref, cb_ref, dtb_ref, alog_ref, dvec_ref, nw_ref,
                exp_h_ref, exp_l_ref, o_ref, buf_ref, st_ref):
    q = SSD_CHUNK

    @pl.when(pl.program_id(1) == 0)
    def _():
        buf_ref[0:SUBLANES, :] = jnp.zeros((SUBLANES, SSD_CONV_DIM), F32)
        st_ref[...] = jnp.zeros_like(st_ref)

    buf_ref[SUBLANES:SUBLANES + q, 0:SSD_INNER] = xs_ref[...]
    buf_ref[SUBLANES:SUBLANES + q, SSD_INNER:SSD_INNER + SSD_BC] = b_ref[...]
    buf_ref[SUBLANES:SUBLANES + q, SSD_INNER + SSD_BC:SSD_CONV_DIM] = c_ref[...]
    acc = jnp.broadcast_to(cb_ref[...], (q, SSD_CONV_DIM))
    for k in range(SSD_CONV):
        acc = acc + cw_ref[k:k + 1, :] * buf_ref[pl.ds(SUBLANES - (SSD_CONV - 1) + k, q), :]
    buf_ref[0:SUBLANES, :] = buf_ref[q:q + SUBLANES, :]
    xbc = acc * jax.nn.sigmoid(acc)
    xs = xbc[:, :SSD_INNER]
    bm = xbc[:, SSD_INNER:SSD_INNER + SSD_BC]
    cm = xbc[:, SSD_INNER + SSD_BC:]

    lane = lax.broadcasted_iota(jnp.int32, (q, LANES), 1)
    row = lax.broadcasted_iota(jnp.int32, (q, LANES), 0)
    head_lane = lane < SSD_HEADS
    dt_in = dt_ref[...] + dtb_ref[...]
    dt = jnp.maximum(dt_in, 0.0) + jnp.log1p(jnp.exp(-jnp.abs(dt_in)))
    dt = jnp.where(head_lane, dt, 0.0)
    a = -jnp.exp(alog_ref[...])
    ad = dt * a
    tri = (row >= lane).astype(BF16)
    cs = _exact_dot_rhs(tri, ad)
    cs_t = cs.T
    dt_t = dt.T
    total = cs[q - 1:q, :]
    dec_t = jnp.exp(cs_t[:, q - 1:q] - cs_t)
    w_t = dec_t * dt_t

    exp_h = exp_h_ref[...]
    exp_l = exp_l_ref[...]
    ecs_x = _exact_dot_lhs(jnp.exp(cs), exp_h)
    w_x = _exact_dot_lhs(w_t.T, exp_h)
    cs_cols = _exact_dot_lhs(cs, exp_l)
    tot_x = _exact_dot_lhs(jnp.broadcast_to(jnp.exp(total), (SUBLANES, LANES)), exp_h)[0:1, :]

    bm16 = bm.astype(BF16)
    cm16 = cm.astype(BF16)
    xs16 = xs.astype(BF16)
    low = lane < SSD_STATE
    g_mats = [_dot_nt(jnp.where(low, cm, 0.0).astype(BF16), bm16),
              _dot_nt(jnp.where(low, 0.0, cm).astype(BF16), bm16)]
    causal = row >= lane
    ydiag = []
    for j in range(SSD_HEADS // 2):
        slab = xs[:, j * LANES:(j + 1) * LANES]
        halves = (jnp.where(low, slab, 0.0).astype(BF16), jnp.where(low, 0.0, slab).astype(BF16))
        acc_j = None
        for half in range(2):
            h = 2 * j + half
            g = h // (SSD_HEADS // SSD_GROUPS)
            seg = cs_cols[:, h * LANES:(h + 1) * LANES] - cs_t[h:h + 1, :]
            lmat = jnp.exp(jnp.where(causal, seg, NEG_INF))
            m = (g_mats[g] * lmat * dt_t[h:h + 1, :]).astype(BF16)
            part = _dot(m, halves[half])
            acc_j = part if acc_j is None else acc_j + part
        ydiag.append(acc_j)
    y = jnp.concatenate(ydiag, axis=1)

    st = st_ref[...]
    y = y + _dot(cm16, st.astype(BF16)) * ecs_x
    new = _dot(bm.T.astype(BF16), (xs * w_x).astype(BF16))
    srow = lax.broadcasted_iota(jnp.int32, (LANES, SSD_INNER), 0)
    scol = lax.broadcasted_iota(jnp.int32, (LANES, SSD_INNER), 1)
    same_group = (srow < SSD_STATE) == (scol < SSD_INNER // SSD_GROUPS)
    st_ref[...] = jnp.where(same_group, st * tot_x + new, 0.0)

    y = y + dvec_ref[...] * xs
    zz = z_ref[...]
    gated = y * (zz * jax.nn.sigmoid(zz))
    out = gated * lax.rsqrt(jnp.mean(gated * gated, axis=-1, keepdims=True) + EPS) * nw_ref[...]
    o_ref[...] = out.astype(o_ref.dtype)


def _ssd(big, small, conv_w, conv_b, dt_bias, a_log, dvec, norm_w, *, batch, seq):
    n = batch * seq
    nc = seq // SSD_CHUNK
    q = SSD_CHUNK
    pad = LANES - SSD_HEADS
    dtb = jnp.pad(dt_bias.astype(F32), (0, pad)).reshape(1, LANES)
    alog = jnp.pad(a_log.astype(F32), (0, pad)).reshape(1, LANES)
    d_x = jnp.repeat(dvec.astype(F32), SSD_HEAD_DIM).reshape(1, SSD_INNER)
    heads = np.arange(LANES)[:, None]
    exp_h = jnp.asarray(heads == (np.arange(SSD_INNER)[None, :] // SSD_HEAD_DIM), dtype=BF16)
    exp_l = jnp.asarray(heads == (np.arange(SSD_HEADS * LANES)[None, :] // LANES), dtype=BF16)

    def blk(col, width):
        return pl.BlockSpec((q, width), lambda b, c: (b * nc + c, col))

    def full(shape):
        return pl.BlockSpec(shape, lambda b, c: (0,) * len(shape))

    return pl.pallas_call(
        _ssd_kernel,
        grid=(batch, nc),
        in_specs=[
            blk(0, SSD_INNER), blk(1, SSD_INNER),
            blk(2, LANES), blk(3, LANES), blk(4, LANES),
            full((SSD_CONV, SSD_CONV_DIM)), full((1, SSD_CONV_DIM)),
            full((1, LANES)), full((1, LANES)), full((1, SSD_INNER)), full((1, SSD_INNER)),
            full((LANES, SSD_INNER)), full((LANES, SSD_HEADS * LANES)),
        ],
        out_specs=pl.BlockSpec((q, SSD_INNER), lambda b, c: (b * nc + c, 0)),
        out_shape=jax.ShapeDtypeStruct((n, SSD_INNER), BF16),
        scratch_shapes=[
            pltpu.VMEM((q + SUBLANES, SSD_CONV_DIM), F32),
            pltpu.VMEM((LANES, SSD_INNER), F32),
        ],
        compiler_params=_params("parallel", "arbitrary"),
        name="ssd",
    )(big, big, small, small, small, conv_w.astype(F32), conv_b.astype(F32).reshape(1, SSD_CONV_DIM),
      dtb, alog, d_x, norm_w.astype(F32).reshape(1, SSD_INNER), exp_h, exp_l)


def _merge_kernel(attn_ref, ys5_ref, ssd_ref, ga_ref, gb_ref, gc_ref, x_ref,
                  wa_ref, wga_ref, wgb_ref, wc_ref, wo_ref, o_ref):
    ys5 = ys5_ref[...]
    br_a = _dot(attn_ref[...], wa_ref[...])
    br_b = _dot(ys5, wga_ref[...]) * jax.nn.sigmoid(_dot(ys5, wgb_ref[...]))
    br_c = _dot(ssd_ref[...], wc_ref[...])
    merged = (jax.nn.sigmoid(ga_ref[...]) * br_a + jax.nn.sigmoid(gb_ref[...]) * br_b
              + jax.nn.sigmoid(gc_ref[...]) * br_c)
    o_ref[...] = x_ref[...] + _dot(merged.astype(BF16), wo_ref[...])


def _merge(attn, ys5, ssd, big, x, wa, wga, wgb, wc, wo, *, tm=256):
    n = x.shape[0]

    def rows(width, col=0):
        return pl.BlockSpec((tm, width), lambda i: (i, col))

    def full(a):
        return pl.BlockSpec(a.shape, lambda i: (0, 0))

    return pl.pallas_call(
        _merge_kernel,
        grid=(n // tm,),
        in_specs=[rows(ATTN_WIDTH), rows(S5_WIDTH), rows(SSD_INNER),
                  rows(D_MODEL, 2), rows(D_MODEL, 3), rows(D_MODEL, 4), rows(D_MODEL),
                  full(wa), full(wga), full(wgb), full(wc), full(wo)],
        out_specs=rows(D_MODEL),
        out_shape=jax.ShapeDtypeStruct((n, D_MODEL), F32),
        compiler_params=_params("parallel"),
        name="merge",
    )(attn, ys5, ssd, big, big, big, x, wa, wga, wgb, wc, wo)


def _xattn_kernel(x_ref, g_ref, kv_ref, wq_ref, wo_ref, o_ref):
    x = x_ref[...]
    h = (x * lax.rsqrt(jnp.mean(x * x, axis=-1, keepdims=True) + EPS) * g_ref[...]).astype(BF16)
    q = (_dot(h, wq_ref[...]) * (XA_HEAD_DIM ** -0.5)).astype(BF16)
    kv = kv_ref[...]
    k = kv[:, :XA_WIDTH]
    v = kv[:, XA_WIDTH:]
    lane = lax.broadcasted_iota(jnp.int32, k.shape, 1)
    o = None
    for hd in range(XA_HEADS):
        sel = (lane >= hd * XA_HEAD_DIM) & (lane < (hd + 1) * XA_HEAD_DIM)
        kh = jnp.where(sel, k, 0.0).astype(BF16)
        vh = jnp.where(sel, v, 0.0).astype(BF16)
        s = _dot_nt(q, kh)
        s = s - jnp.max(s, axis=-1, keepdims=True)
        p = jnp.exp(s)
        p = p / jnp.sum(p, axis=-1, keepdims=True)
        part = _dot(p.astype(BF16), vh)
        o = part if o is None else o + part
    o_ref[...] = x + _dot(o.astype(BF16), wo_ref[...])


def _xattn(x, g, kv, wq, wo, *, batch, seq, n_mem, tm=256):
    n = batch * seq
    nt = seq // tm
    return pl.pallas_call(
        _xattn_kernel,
        grid=(batch, nt),
        in_specs=[
            pl.BlockSpec((tm, D_MODEL), lambda b, i: (b * nt + i, 0)),
            pl.BlockSpec((1, D_MODEL), lambda b, i: (0, 0)),
            pl.BlockSpec((n_mem, 2 * XA_WIDTH), lambda b, i: (b, 0)),
            pl.BlockSpec((D_MODEL, XA_WIDTH), lambda b, i: (0, 0)),
            pl.BlockSpec((XA_WIDTH, D_MODEL), lambda b, i: (0, 0)),
        ],
        out_specs=pl.BlockSpec((tm, D_MODEL), lambda b, i: (b * nt + i, 0)),
        out_shape=jax.ShapeDtypeStruct((n, D_MODEL), F32),
        compiler_params=_params("parallel", "parallel"),
        name="xattn",
    )(x, g.reshape(1, D_MODEL), kv, wq, wo)


_PEER_CELLS = [(i, j) for i in range(PEER_TOPK) for j in range(PEER_TOPK) if (i + 1) * (j + 1) <= PEER_TOPK]


def _peer_route_kernel(q_ref, k1_ref, k2_ref, e1_ref, e2_ref, thr_ref, rows_ref, kept_ref):
    tm = q_ref.shape[0]
    keys = (k1_ref[...], k2_ref[...])
    kidx = lax.broadcasted_iota(jnp.int32, (PEER_N_KEYS, tm), 0).astype(F32)

    def store(which, hd, st, rows, vals):
        for r in range(PEER_TOPK):
            rows_ref[which, r, hd:hd + 1, :] = rows[r] - rows[0]
        kept_ref[which, hd] = jnp.where(vals == NEG_INF, st - rows[0], NEG_INF)

    def scores(hd, which):
        c0 = hd * PEER_KEY_DIM + which * PEER_HALF
        return _dot_nt(keys[which], q_ref[:, c0:c0 + PEER_HALF])

    tied = jnp.zeros((1, tm), F32)
    for hd in range(PEER_HEADS):
        for which in range(2):
            st = scores(hd, which)
            vals = st
            rows = []
            for _ in range(PEER_TOPK):
                m = jnp.max(vals, axis=0, keepdims=True)
                vals = jnp.where(vals == m, NEG_INF, vals)
                rows.append(m)
            store(which, hd, st, rows, vals)
            removed = jnp.sum(jnp.where(vals == NEG_INF, 1.0, 0.0), axis=0, keepdims=True)
            tied = jnp.maximum(tied, jnp.abs(removed - float(PEER_TOPK)))

    @pl.when(jnp.max(tied) > 0.0)
    def _():
        for hd in range(PEER_HEADS):
            for which in range(2):
                st = scores(hd, which)
                vals = st
                rows = []
                for _ in range(PEER_TOPK):
                    m = jnp.max(vals, axis=0, keepdims=True)
                    first = jnp.min(jnp.where(vals == m, kidx, float(PEER_N_KEYS)), axis=0, keepdims=True)
                    vals = jnp.where(kidx == first, NEG_INF, vals)
                    rows.append(m)
                store(which, hd, st, rows, vals)

    v1 = [rows_ref[0, r] for r in range(PEER_TOPK)]
    v2 = [rows_ref[1, r] for r in range(PEER_TOPK)]
    cands = [v1[i] + v2[j] for (i, j) in _PEER_CELLS]
    work = cands
    cum = jnp.zeros_like(v1[0])
    tau = jnp.full_like(v1[0], NEG_INF)
    zsum = jnp.zeros_like(v1[0])
    for _ in range(PEER_TOPK):
        m = functools.reduce(jnp.maximum, work)
        cnt = jnp.zeros_like(m)
        nxt = []
        for c in work:
            eq = c == m
            cnt = cnt + jnp.where(eq, 1.0, 0.0)
            nxt.append(jnp.where(eq, NEG_INF, c))
        work = nxt
        open_ = cum < PEER_TOPK
        used = jnp.minimum(cnt, PEER_TOPK - cum)
        zsum = zsum + jnp.where(open_, used * jnp.exp(m), 0.0)
        tau = jnp.where(open_, m, tau)
        cum = cum + cnt
    log_z = jnp.log(zsum)
    e1v = [jnp.exp(v - log_z) for v in v1]
    e2v = [jnp.exp(v) for v in v2]
    thr = jnp.full_like(tau, jnp.inf)
    for (i, j), c in zip(_PEER_CELLS, cands):
        thr = jnp.minimum(thr, jnp.where(c >= tau, e1v[i] * e2v[j], jnp.inf))
    thr_ref[0] = thr
    for hd in range(PEER_HEADS):
        e1_ref[hd, 0] = jnp.exp(kept_ref[0, hd] - log_z[hd:hd + 1, :])
        e2_ref[hd, 0] = jnp.exp(kept_ref[1, hd])


def _peer_route(q, k1, k2):
    n = q.shape[0]
    tm = LANES
    ns = n // tm
    big_shape = jax.ShapeDtypeStruct((PEER_HEADS, ns, PEER_N_KEYS, tm), F32)
    big_spec = pl.BlockSpec((PEER_HEADS, 1, PEER_N_KEYS, tm), lambda i: (0, i, 0, 0))
    return pl.pallas_call(
        _peer_route_kernel,
        grid=(ns,),
        in_specs=[
            pl.BlockSpec((tm, PEER_HEADS * PEER_KEY_DIM), lambda i: (i, 0)),
            pl.BlockSpec((PEER_N_KEYS, PEER_HALF), lambda i: (0, 0)),
            pl.BlockSpec((PEER_N_KEYS, PEER_HALF), lambda i: (0, 0)),
        ],
        out_specs=[big_spec, big_spec, pl.BlockSpec((1, PEER_HEADS, tm), lambda i: (i, 0, 0))],
        out_shape=[big_shape, big_shape, jax.ShapeDtypeStruct((ns, PEER_HEADS, tm), F32)],
        scratch_shapes=[
            pltpu.VMEM((2, PEER_TOPK, PEER_HEADS, tm), F32),
            pltpu.VMEM((2, PEER_HEADS, PEER_N_KEYS, tm), F32),
        ],
        compiler_params=_params("parallel"),
        name="peer_route",
    )(q, k1, k2)


PEER_TE = 512
PEER_TM = 1024
PEER_STRIP = 256


def _peer_dense_kernel(x_ref, ht_ref, e1_ref, e2_ref, thr_ref, u_ref, vt_ref, o_ref, acc_ref, act_ref, p_ref):
    j = pl.program_id(1)
    n_slab = PEER_TE // PEER_N_KEYS

    @pl.when(j == 0)
    def _():
        acc_ref[...] = jnp.zeros_like(acc_ref)

    u = u_ref[...]
    vt = vt_ref[...]
    wide = PEER_STRIP
    n_strip = PEER_TM // wide

    def first_matmul(s):
        return _dot(u, ht_ref[s])

    act_ref[0] = first_matmul(0)
    for s in range(n_strip):
        slot = s % 2
        if s + 1 < n_strip:
            act_ref[1 - slot] = first_matmul(s + 1)
        if s >= 1:
            acc_ref[s - 1] += _dot(vt, p_ref[1 - slot])
        for half in range(wide // LANES):
            tc = s * (wide // LANES) + half
            thr = thr_ref[tc]
            for al in range(n_slab):
                a = j * n_slab + al
                w = None
                for hd in range(PEER_HEADS):
                    prod = e1_ref[hd, tc, pl.ds(a, 1), :] * e2_ref[hd, tc]
                    contrib = jnp.where(prod >= thr[hd:hd + 1, :], prod, 0.0)
                    w = contrib if w is None else w + contrib
                rows = slice(al * PEER_N_KEYS, (al + 1) * PEER_N_KEYS)
                lanes = slice(half * LANES, (half + 1) * LANES)
                g = jax.nn.gelu(act_ref[slot, rows, lanes])
                p_ref[slot, rows, lanes] = (w * g).astype(BF16)
    acc_ref[n_strip - 1] += _dot(vt, p_ref[(n_strip - 1) % 2])

    @pl.when(j == pl.num_programs(1) - 1)
    def _():
        for s in range(n_strip):
            o_ref[s * wide:(s + 1) * wide, :] = x_ref[s * wide:(s + 1) * wide, :] + acc_ref[s].T


def _peer_dense(x, ht, e1, e2, thr, u, vt):
    n = x.shape[0]
    tm, te = PEER_TM, PEER_TE
    ns = tm // LANES
    big_spec = pl.BlockSpec((PEER_HEADS, ns, PEER_N_KEYS, LANES), lambda i, j: (0, i, 0, 0))
    return pl.pallas_call(
        _peer_dense_kernel,
        grid=(n // tm, PEER_N_EXPERTS // te),
        in_specs=[
            pl.BlockSpec((tm, D_MODEL), lambda i, j: (i, 0)),
            pl.BlockSpec((tm // PEER_STRIP, D_MODEL, PEER_STRIP), lambda i, j: (i, 0, 0)),
            big_spec, big_spec,
            pl.BlockSpec((ns, PEER_HEADS, LANES), lambda i, j: (i, 0, 0)),
            pl.BlockSpec((te, D_MODEL), lambda i, j: (j, 0)),
            pl.BlockSpec((D_MODEL, te), lambda i, j: (0, j)),
        ],
        out_specs=pl.BlockSpec((tm, D_MODEL), lambda i, j: (i, 0)),
        out_shape=jax.ShapeDtypeStruct((n, D_MODEL), F32),
        scratch_shapes=[
            pltpu.VMEM((tm // PEER_STRIP, D_MODEL, PEER_STRIP), F32),
            pltpu.VMEM((2, te, PEER_STRIP), F32),
            pltpu.VMEM((2, te, PEER_STRIP), BF16),
        ],
        compiler_params=_params("parallel", "arbitrary"),
        name="peer_dense",
    )(x, ht, e1, e2, thr, u, vt)


def _final_norm_kernel(x_ref, g_ref, o_ref):
    x = x_ref[...]
    o_ref[...] = x * lax.rsqrt(jnp.mean(x * x, axis=-1, keepdims=True) + EPS) * g_ref[...]


def _final_norm(x, g, *, tm=512):
    n, d = x.shape
    return pl.pallas_call(
        _final_norm_kernel,
        grid=(n // tm,),
        in_specs=[pl.BlockSpec((tm, d), lambda i: (i, 0)), pl.BlockSpec((1, d), lambda i: (0, 0))],
        out_specs=pl.BlockSpec((tm, d), lambda i: (i, 0)),
        out_shape=jax.ShapeDtypeStruct((n, d), F32),
        compiler_params=_params("parallel"),
        name="final_norm",
    )(x, g.reshape(1, d))


def _q_head_perm():
    cols = []
    for j in range(ATTN_HEADS // 2):
        for hd in (j, j + ATTN_HEADS // 2):
            cols.extend(range(hd * ATTN_HEAD_DIM, (hd + 1) * ATTN_HEAD_DIM))
    return np.asarray(cols, dtype=np.int32)


def _split_w_in(w):
    o = 0
    q = w[:, o:o + 512]; o += 512
    k = w[:, o:o + 128]; o += 128
    v = w[:, o:o + 128]; o += 128
    u = w[:, o:o + 512]; o += 512
    z = w[:, o:o + 1024]; o += 1024
    xs = w[:, o:o + 1024]; o += 1024
    bm = w[:, o:o + 128]; o += 128
    cm = w[:, o:o + 128]; o += 128
    dt = w[:, o:o + 16]; o += 16
    gates = w[:, o:o + 3072]
    q = q[:, _q_head_perm()]
    big = jnp.concatenate([z, xs, gates, q, u], axis=1).astype(BF16)
    small = jnp.concatenate([k, v, bm, cm, dt, jnp.zeros((w.shape[0], LANES - SSD_HEADS), w.dtype)], axis=1)
    return big, small.astype(BF16)


def kernel(x, mem, norm_mix, w_in, attn_sinks, s5_lambda_re, s5_lambda_im, s5_log_dt, s5_b_re, s5_b_im, s5_c_re, s5_c_im, s5_d, s5_glu_a, s5_glu_b, ssd_conv_w, ssd_conv_b, ssd_dt_bias, ssd_a_log, ssd_d, ssd_norm, w_attn_out, w_ssd_out, w_o, norm_xattn, norm_mem, xa_wq, xa_wkv, xa_wo, norm_ffn, peer_wq, peer_k1, peer_k2, peer_u, peer_v, norm_final):
    batch, seq, d = x.shape
    n = batch * seq
    n_mem = mem.shape[1]
    depth = w_in.shape[0]
    xf = x.reshape(n, d).astype(F32)
    memf = mem.reshape(batch * n_mem, d).astype(F32)
    for i in range(depth):
        w_big, w_small = _split_w_in(w_in[i])
        big = _norm_mm(xf, norm_mix[i], w_big, tm=1024, tn=512, out_dtype=F32, name="in_proj_big")
        small = _norm_mm(xf, norm_mix[i], w_small, tm=1024, tn=SMALL_COLS, out_dtype=F32, name="in_proj_small")
        attn = _swa(big, small, attn_sinks[i].astype(F32), batch=batch, seq=seq)
        wb, wc, sc = _s5_prepare(s5_lambda_re[i], s5_lambda_im[i], s5_log_dt[i], s5_b_re[i], s5_b_im[i],
                                 s5_c_re[i], s5_c_im[i])
        ys5 = _s5(big, wb, wc, s5_d[i].astype(F32), sc, batch=batch, seq=seq)
        ssd = _ssd(big, small, ssd_conv_w[i], ssd_conv_b[i], ssd_dt_bias[i], ssd_a_log[i], ssd_d[i], ssd_norm[i],
                   batch=batch, seq=seq)
        xf = _merge(attn, ys5, ssd, big, xf,
                    w_attn_out[i][_q_head_perm(), :].astype(BF16), s5_glu_a[i].astype(BF16),
                    s5_glu_b[i].astype(BF16), w_ssd_out[i].astype(BF16), w_o[i].astype(BF16))
        kv = _norm_mm(memf, norm_mem[i], xa_wkv[i].astype(BF16), tm=batch * n_mem, tn=2 * XA_WIDTH,
                      out_dtype=F32, name="mem_kv")
        xf = _xattn(xf, norm_xattn[i], kv, xa_wq[i].astype(BF16), xa_wo[i].astype(BF16),
                    batch=batch, seq=seq, n_mem=n_mem)
        q, ht = _norm_mm(xf, norm_ffn[i], peer_wq[i].astype(BF16), tm=1024, tn=512, out_dtype=BF16,
                        with_h=True, name="peer_query")
        e1, e2, thr = _peer_route(q, peer_k1[i].astype(BF16), peer_k2[i].astype(BF16))
        xf = _peer_dense(xf, ht, e1, e2, thr, peer_u[i].astype(BF16), peer_v[i].astype(BF16).T)
    return _final_norm(xf, norm_final.astype(F32)).reshape(batch, seq, d)
```

```python
import functools
import math

import jax
import jax.numpy as jnp
import numpy as np
from jax import lax
from jax.experimental import pallas as pl
from jax.experimental.pallas import tpu as pltpu

F32 = jnp.float32
BF16 = jnp.bfloat16
NEG_INF = float("-inf")

D_MODEL = 1024
EPS = 1e-6
ATTN_HEADS = 8
ATTN_HEAD_DIM = 64
ATTN_WIDTH = ATTN_HEADS * ATTN_HEAD_DIM
ATTN_BLOCK = 128
S5_WIDTH = 512
S5_GROUP = 16
S5_GROUPS = S5_WIDTH // S5_GROUP
S5_STATE = 64
S5_LANES = S5_GROUPS * S5_STATE
S5_CHUNK = 256
S5_SCAN_WIDTH = 512
SSD_INNER = 1024
SSD_HEAD_DIM = 64
SSD_HEADS = SSD_INNER // SSD_HEAD_DIM
SSD_STATE = 64
SSD_GROUPS = 2
SSD_CONV = 4
SSD_CHUNK = 128
SSD_BC = SSD_GROUPS * SSD_STATE
SSD_CONV_DIM = SSD_INNER + 2 * SSD_BC
XA_HEADS = 4
XA_HEAD_DIM = 64
XA_WIDTH = XA_HEADS * XA_HEAD_DIM
PEER_HEADS = 8
PEER_KEY_DIM = 256
PEER_HALF = PEER_KEY_DIM // 2
PEER_N_KEYS = 128
PEER_TOPK = 16
PEER_N_EXPERTS = PEER_N_KEYS * PEER_N_KEYS
LANES = 128
SUBLANES = 8
VMEM_LIMIT_BYTES = 56 * 1024 * 1024

BIG_COLS = 6144
SMALL_COLS = 640


def _params(*sem):
    return pltpu.CompilerParams(dimension_semantics=sem, vmem_limit_bytes=VMEM_LIMIT_BYTES)


def _dot(a, b):
    return jnp.dot(a, b, preferred_element_type=F32)


def _dot_nt(a, b):
    return lax.dot_general(a, b, (((1,), (1,)), ((), ())), preferred_element_type=F32)


def _split3(x):
    p1 = x.astype(BF16)
    r1 = x - p1.astype(F32)
    p2 = r1.astype(BF16)
    p3 = (r1 - p2.astype(F32)).astype(BF16)
    return p1, p2, p3


def _exact_dot_rhs(sel, x):
    p1, p2, p3 = _split3(x)
    return _dot(sel, p1) + _dot(sel, p2) + _dot(sel, p3)


def _exact_dot_lhs(x, sel):
    p1, p2, p3 = _split3(x)
    return _dot(p1, sel) + _dot(p2, sel) + _dot(p3, sel)


def _norm_mm_kernel(x_ref, g_ref, w_ref, o_ref, *rest, with_h):
    if with_h:
        hout_ref, h_ref = rest
    else:
        (h_ref,) = rest

    @pl.when(pl.program_id(1) == 0)
    def _():
        x = x_ref[...]
        h = x * lax.rsqrt(jnp.mean(x * x, axis=-1, keepdims=True) + EPS) * g_ref[...]
        h_ref[...] = h.astype(BF16)
        if with_h:
            for s in range(hout_ref.shape[0]):
                hout_ref[s] = h[s * PEER_STRIP:(s + 1) * PEER_STRIP, :].T.astype(BF16)

    o_ref[...] = _dot(h_ref[...], w_ref[...]).astype(o_ref.dtype)


def _norm_mm(x, g, w, *, tm, tn, out_dtype, with_h=False, name):
    n, k = x.shape
    m = w.shape[1]
    assert n % tm == 0 and m % tn == 0
    out_shape = [jax.ShapeDtypeStruct((n, m), out_dtype)]
    out_specs = [pl.BlockSpec((tm, tn), lambda i, j: (i, j))]
    if with_h:
        out_shape.append(jax.ShapeDtypeStruct((n // PEER_STRIP, k, PEER_STRIP), BF16))
        out_specs.append(pl.BlockSpec((tm // PEER_STRIP, k, PEER_STRIP), lambda i, j: (i, 0, 0)))
    res = pl.pallas_call(
        functools.partial(_norm_mm_kernel, with_h=with_h),
        grid=(n // tm, m // tn),
        in_specs=[
            pl.BlockSpec((tm, k), lambda i, j: (i, 0)),
            pl.BlockSpec((1, k), lambda i, j: (0, 0)),
            pl.BlockSpec((k, tn), lambda i, j: (0, j)),
        ],
        out_specs=out_specs,
        out_shape=out_shape,
        scratch_shapes=[pltpu.VMEM((tm, k), BF16)],
        compiler_params=_params("parallel", "arbitrary"),
        name=name,
    )(x, g.reshape(1, k), w)
    return res if with_h else res[0]


def _swa_kernel(sink_ref, q_ref, kc_ref, kp_ref, vc_ref, vp_ref, o_ref):
    nblk = pl.program_id(1)
    blk = ATTN_BLOCK
    q = q_ref[...]
    k = jnp.concatenate([kp_ref[...], kc_ref[...]], axis=0)
    v = jnp.concatenate([vp_ref[...], vc_ref[...]], axis=0).astype(BF16)
    lane_k = lax.broadcasted_iota(jnp.int32, (2 * blk, LANES), 1)
    k_lo = jnp.where(lane_k < ATTN_HEAD_DIM, k, 0.0).astype(BF16)
    k_hi = jnp.where(lane_k < ATTN_HEAD_DIM, 0.0, k).astype(BF16)
    qi = lax.broadcasted_iota(jnp.int32, (blk, 2 * blk), 0)
    kj = lax.broadcasted_iota(jnp.int32, (blk, 2 * blk), 1)
    dist = qi - kj + blk
    valid = (dist >= 0) & (dist < blk) & ((nblk > 0) | (kj >= blk))
    distf = dist.astype(F32)
    lane_o = lax.broadcasted_iota(jnp.int32, (blk, LANES), 1)
    outs = []
    for j in range(ATTN_HEADS // 2):
        qs = (q[:, j * LANES:(j + 1) * LANES] * (ATTN_HEAD_DIM ** -0.5)).astype(BF16)
        res = []
        for half, kk in ((0, k_lo), (1, k_hi)):
            head = j + 4 * half
            slope = 2.0 ** (-8.0 * (head + 1) / ATTN_HEADS)
            s = _dot_nt(qs, kk) - slope * distf
            s = jnp.where(valid, s, NEG_INF)
            sink = sink_ref[head]
            m = jnp.maximum(jnp.max(s, axis=-1, keepdims=True), sink)
            p = jnp.exp(s - m)
            denom = jnp.sum(p, axis=-1, keepdims=True) + jnp.exp(sink - m)
            p = p / denom
            res.append(_dot(p.astype(BF16), v))
        outs.append(jnp.where(lane_o < ATTN_HEAD_DIM, res[0], res[1]))
    o_ref[...] = jnp.concatenate(outs, axis=1).astype(o_ref.dtype)


def _swa(big, small, sinks, *, batch, seq):
    nb = seq // ATTN_BLOCK
    n = batch * seq
    qcol = 5120 // ATTN_WIDTH

    def cur(col):
        return lambda b, i: (b * nb + i, col)

    def prev(col):
        return lambda b, i: (b * nb + jnp.maximum(i - 1, 0), col)

    return pl.pallas_call(
        _swa_kernel,
        grid=(batch, nb),
        in_specs=[
            pl.BlockSpec(memory_space=pltpu.SMEM),
            pl.BlockSpec((ATTN_BLOCK, ATTN_WIDTH), cur(qcol)),
            pl.BlockSpec((ATTN_BLOCK, LANES), cur(0)),
            pl.BlockSpec((ATTN_BLOCK, LANES), prev(0)),
            pl.BlockSpec((ATTN_BLOCK, LANES), cur(1)),
            pl.BlockSpec((ATTN_BLOCK, LANES), prev(1)),
        ],
        out_specs=pl.BlockSpec((ATTN_BLOCK, ATTN_WIDTH), lambda b, i: (b * nb + i, 0)),
        out_shape=jax.ShapeDtypeStruct((n, ATTN_WIDTH), BF16),
        compiler_params=_params("parallel", "arbitrary"),
        name="swa",
    )(sinks, big, small, small, small, small)


def _s5_kernel(u_ref, wb_ref, wc_ref, d_ref, sc_ref, o_ref, xr_ref, xi_ref, cr_ref, ci_ref):
    @pl.when(pl.program_id(1) == 0)
    def _():
        cr_ref[...] = jnp.zeros_like(cr_ref)
        ci_ref[...] = jnp.zeros_like(ci_ref)

    u = u_ref[...].astype(F32)
    bu = _dot(u_ref[...], wb_ref[...])
    xr_ref[...] = bu[:, :S5_LANES]
    xi_ref[...] = bu[:, S5_LANES:]

    def cmul_add(xr, xi, ar, ai, sr, si):
        return xr + (ar * sr - ai * si), xi + (ar * si + ai * sr)

    for w in range(S5_LANES // S5_SCAN_WIDTH):
        sl = slice(w * S5_SCAN_WIDTH, (w + 1) * S5_SCAN_WIDTH)
        consts = [sc_ref[c, :, sl] for c in range(8)]
        a1r, a1i, a2r, a2i, a4r, a4i, pr, pi = consts

        def body(t, carry):
            cr, ci = carry
            r0 = pl.multiple_of(t * SUBLANES, SUBLANES)
            xr = xr_ref[pl.ds(r0, SUBLANES), sl]
            xi = xi_ref[pl.ds(r0, SUBLANES), sl]
            xr, xi = cmul_add(xr, xi, a1r, a1i, pltpu.roll(xr, 1, 0), pltpu.roll(xi, 1, 0))
            xr, xi = cmul_add(xr, xi, a2r, a2i, pltpu.roll(xr, 2, 0), pltpu.roll(xi, 2, 0))
            xr, xi = cmul_add(xr, xi, a4r, a4i, pltpu.roll(xr, 4, 0), pltpu.roll(xi, 4, 0))
            xr, xi = cmul_add(xr, xi, pr, pi, cr, ci)
            xr_ref[pl.ds(r0, SUBLANES), sl] = xr
            xi_ref[pl.ds(r0, SUBLANES), sl] = xi
            return xr[SUBLANES - 1:SUBLANES, :], xi[SUBLANES - 1:SUBLANES, :]

        cr, ci = lax.fori_loop(0, S5_CHUNK // SUBLANES, body, (cr_ref[0:1, sl], ci_ref[0:1, sl]))
        cr_ref[0:1, sl] = cr
        ci_ref[0:1, sl] = ci

    xcat = jnp.concatenate([xr_ref[...].astype(BF16), xi_ref[...].astype(BF16)], axis=1)
    y = _dot(xcat, wc_ref[...]) + d_ref[...] * u
    o_ref[...] = jax.nn.gelu(y).astype(o_ref.dtype)


def _s5_prepare(lam_re, lam_im, log_dt, b_re, b_im, c_re, c_im):
    dt = jnp.exp(log_dt.astype(F32))[:, None]
    lr = lam_re.astype(F32)
    li = lam_im.astype(F32)
    mag = jnp.exp(lr * dt)
    ar = mag * jnp.cos(li * dt)
    ai = mag * jnp.sin(li * dt)
    den = lr * lr + li * li
    nr = ar - 1.0
    wr = (nr * lr + ai * li) / den
    wi = (ai * lr - nr * li) / den
    br_, bi_ = b_re.astype(F32), b_im.astype(F32)
    bbr = wr[..., None] * br_ - wi[..., None] * bi_
    bbi = wr[..., None] * bi_ + wi[..., None] * br_
    eye = jnp.eye(S5_GROUPS, dtype=F32)
    wbr = jnp.einsum('gpi,gh->gihp', bbr, eye).reshape(S5_WIDTH, S5_LANES)
    wbi = jnp.einsum('gpi,gh->gihp', bbi, eye).reshape(S5_WIDTH, S5_LANES)
    wb = jnp.concatenate([wbr, wbi], axis=1).astype(BF16)
    wcr = jnp.einsum('gip,gh->gphi', c_re.astype(F32), eye).reshape(S5_LANES, S5_WIDTH)
    wci = jnp.einsum('gip,gh->gphi', c_im.astype(F32), eye).reshape(S5_LANES, S5_WIDTH)
    wc = jnp.concatenate([wcr, -wci], axis=0).astype(BF16)
    ar_f = ar.reshape(1, S5_LANES)
    ai_f = ai.reshape(1, S5_LANES)

    def cmul(xr, xi, yr, yi):
        return xr * yr - xi * yi, xr * yi + xi * yr

    pows_r, pows_i = [ar_f], [ai_f]
    for _ in range(SUBLANES - 1):
        nr_, ni_ = cmul(pows_r[-1], pows_i[-1], ar_f, ai_f)
        pows_r.append(nr_)
        pows_i.append(ni_)
    t_idx = jnp.arange(SUBLANES)[:, None]

    def masked(k):
        return (jnp.where(t_idx >= k, pows_r[k - 1], 0.0), jnp.where(t_idx >= k, pows_i[k - 1], 0.0))

    a1 = masked(1)
    a2 = masked(2)
    a4 = masked(4)
    pr = jnp.concatenate(pows_r, axis=0)
    pi = jnp.concatenate(pows_i, axis=0)
    sc = jnp.stack([a1[0], a1[1], a2[0], a2[1], a4[0], a4[1], pr, pi], axis=0)
    return wb, wc, sc


def _s5(big, wb, wc, d, sc, *, batch, seq):
    n = batch * seq
    nc = seq // S5_CHUNK
    ucol = 5632 // S5_WIDTH
    return pl.pallas_call(
        _s5_kernel,
        grid=(batch, nc),
        in_specs=[
            pl.BlockSpec((S5_CHUNK, S5_WIDTH), lambda b, c: (b * nc + c, ucol)),
            pl.BlockSpec((S5_WIDTH, 2 * S5_LANES), lambda b, c: (0, 0)),
            pl.BlockSpec((2 * S5_LANES, S5_WIDTH), lambda b, c: (0, 0)),
            pl.BlockSpec((1, S5_WIDTH), lambda b, c: (0, 0)),
            pl.BlockSpec((8, SUBLANES, S5_LANES), lambda b, c: (0, 0, 0)),
        ],
        out_specs=pl.BlockSpec((S5_CHUNK, S5_WIDTH), lambda b, c: (b * nc + c, 0)),
        out_shape=jax.ShapeDtypeStruct((n, S5_WIDTH), BF16),
        scratch_shapes=[
            pltpu.VMEM((S5_CHUNK, S5_LANES), F32),
            pltpu.VMEM((S5_CHUNK, S5_LANES), F32),
            pltpu.VMEM((SUBLANES, S5_LANES), F32),
            pltpu.VMEM((SUBLANES, S5_LANES), F32),
        ],
        compiler_params=_params("parallel", "arbitrary"),
        name="s5",
    )(big, wb, wc, d.reshape(1, S5_WIDTH), sc)


def _ssd_kernel(z_ref, xs_ref, b_ref, c_ref, dt_ref, cw_ref, cb_ref, dtb_ref, alog_ref, dvec_ref, nw_ref,
                exp_h_ref, exp_l_ref, o_ref, buf_ref, st_ref):
    q = SSD_CHUNK

    @pl.when(pl.program_id(1) == 0)
    def _():
        buf_ref[0:SUBLANES, :] = jnp.zeros((SUBLANES, SSD_CONV_DIM), F32)
        st_ref[...] = jnp.zeros_like(st_ref)

    buf_ref[SUBLANES:SUBLANES + q, 0:SSD_INNER] = xs_ref[...].astype(F32)
    buf_ref[SUBLANES:SUBLANES + q, SSD_INNER:SSD_INNER + SSD_BC] = b_ref[...]
    buf_ref[SUBLANES:SUBLANES + q, SSD_INNER + SSD_BC:SSD_CONV_DIM] = c_ref[...]
    acc = jnp.broadcast_to(cb_ref[...], (q, SSD_CONV_DIM))
    for k in range(SSD_CONV):
        acc = acc + cw_ref[k:k + 1, :] * buf_ref[pl.ds(SUBLANES - (SSD_CONV - 1) + k, q), :]
    buf_ref[0:SUBLANES, :] = buf_ref[q:q + SUBLANES, :]
    xbc = acc * jax.nn.sigmoid(acc)
    xs = xbc[:, :SSD_INNER]
    bm = xbc[:, SSD_INNER:SSD_INNER + SSD_BC]
    cm = xbc[:, SSD_INNER + SSD_BC:]

    lane = lax.broadcasted_iota(jnp.int32, (q, LANES), 1)
    row = lax.broadcasted_iota(jnp.int32, (q, LANES), 0)
    head_lane = lane < SSD_HEADS
    dt_in = dt_ref[...] + dtb_ref[...]
    dt = jnp.maximum(dt_in, 0.0) + jnp.log1p(jnp.exp(-jnp.abs(dt_in)))
    dt = jnp.where(head_lane, dt, 0.0)
    a = -jnp.exp(alog_ref[...])
    ad = dt * a
    tri = (row >= lane).astype(BF16)
    cs = _exact_dot_rhs(tri, ad)
    cs_t = cs.T
    dt_t = dt.T
    total = cs[q - 1:q, :]
    dec_t = jnp.exp(cs_t[:, q - 1:q] - cs_t)
    w_t = dec_t * dt_t

    exp_h = exp_h_ref[...]
    exp_l = exp_l_ref[...]
    ecs_x = _exact_dot_lhs(jnp.exp(cs), exp_h)
    w_x = _exact_dot_lhs(w_t.T, exp_h)
    cs_cols = _exact_dot_lhs(cs, exp_l)
    tot_x = _exact_dot_lhs(jnp.broadcast_to(jnp.exp(total), (SUBLANES, LANES)), exp_h)[0:1, :]

    bm16 = bm.astype(BF16)
    cm16 = cm.astype(BF16)
    xs16 = xs.astype(BF16)
    low = lane < SSD_STATE
    g_mats = [_dot_nt(jnp.where(low, cm, 0.0).astype(BF16), bm16),
              _dot_nt(jnp.where(low, 0.0, cm).astype(BF16), bm16)]
    causal = row >= lane
    ydiag = []
    for j in range(SSD_HEADS // 2):
        slab = xs[:, j * LANES:(j + 1) * LANES]
        halves = (jnp.where(low, slab, 0.0).astype(BF16), jnp.where(low, 0.0, slab).astype(BF16))
        acc_j = None
        for half in range(2):
            h = 2 * j + half
            g = h // (SSD_HEADS // SSD_GROUPS)
            seg = cs_cols[:, h * LANES:(h + 1) * LANES] - cs_t[h:h + 1, :]
            lmat = jnp.exp(jnp.where(causal, seg, NEG_INF))
            m = (g_mats[g] * lmat * dt_t[h:h + 1, :]).astype(BF16)
            part = _dot(m, halves[half])
            acc_j = part if acc_j is None else acc_j + part
        ydiag.append(acc_j)
    y = jnp.concatenate(ydiag, axis=1)

    st = st_ref[...]
    y = y + _dot(cm16, st.astype(BF16)) * ecs_x
    new = _dot(bm.T.astype(BF16), (xs * w_x).astype(BF16))
    srow = lax.broadcasted_iota(jnp.int32, (LANES, SSD_INNER), 0)
    scol = lax.broadcasted_iota(jnp.int32, (LANES, SSD_INNER), 1)
    same_group = (srow < SSD_STATE) == (scol < SSD_INNER // SSD_GROUPS)
    st_ref[...] = jnp.where(same_group, st * tot_x + new, 0.0)

    y = y + dvec_ref[...] * xs
    zz = z_ref[...].astype(F32)
    gated = y * (zz * jax.nn.sigmoid(zz))
    out = gated * lax.rsqrt(jnp.mean(gated * gated, axis=-1, keepdims=True) + EPS) * nw_ref[...]
    o_ref[...] = out.astype(o_ref.dtype)


def _ssd(big, small, conv_w, conv_b, dt_bias, a_log, dvec, norm_w, *, batch, seq):
    n = batch * seq
    nc = seq // SSD_CHUNK
    q = SSD_CHUNK
    pad = LANES - SSD_HEADS
    dtb = jnp.pad(dt_bias.astype(F32), (0, pad)).reshape(1, LANES)
    alog = jnp.pad(a_log.astype(F32), (0, pad)).reshape(1, LANES)
    d_x = jnp.repeat(dvec.astype(F32), SSD_HEAD_DIM).reshape(1, SSD_INNER)
    heads = np.arange(LANES)[:, None]
    exp_h = jnp.asarray(heads == (np.arange(SSD_INNER)[None, :] // SSD_HEAD_DIM), dtype=BF16)
    exp_l = jnp.asarray(heads == (np.arange(SSD_HEADS * LANES)[None, :] // LANES), dtype=BF16)

    def blk(col, width):
        return pl.BlockSpec((q, width), lambda b, c: (b * nc + c, col))

    def full(shape):
        return pl.BlockSpec(shape, lambda b, c: (0,) * len(shape))

    return pl.pallas_call(
        _ssd_kernel,
        grid=(batch, nc),
        in_specs=[
            blk(0, SSD_INNER), blk(1, SSD_INNER),
            blk(2, LANES), blk(3, LANES), blk(4, LANES),
            full((SSD_CONV, SSD_CONV_DIM)), full((1, SSD_CONV_DIM)),
            full((1, LANES)), full((1, LANES)), full((1, SSD_INNER)), full((1, SSD_INNER)),
            full((LANES, SSD_INNER)), full((LANES, SSD_HEADS * LANES)),
        ],
        out_specs=pl.BlockSpec((q, SSD_INNER), lambda b, c: (b * nc + c, 0)),
        out_shape=jax.ShapeDtypeStruct((n, SSD_INNER), BF16),
        scratch_shapes=[
            pltpu.VMEM((q + SUBLANES, SSD_CONV_DIM), F32),
            pltpu.VMEM((LANES, SSD_INNER), F32),
        ],
        compiler_params=_params("parallel", "arbitrary"),
        name="ssd",
    )(big, big, small, small, small, conv_w.astype(F32), conv_b.astype(F32).reshape(1, SSD_CONV_DIM),
      dtb, alog, d_x, norm_w.astype(F32).reshape(1, SSD_INNER), exp_h, exp_l)


def _merge_kernel(attn_ref, ys5_ref, ssd_ref, ga_ref, gb_ref, gc_ref, x_ref,
                  wa_ref, wga_ref, wgb_ref, wc_ref, wo_ref, o_ref):
    ys5 = ys5_ref[...]
    br_a = _dot(attn_ref[...], wa_ref[...])
    br_b = _dot(ys5, wga_ref[...]) * jax.nn.sigmoid(_dot(ys5, wgb_ref[...]))
    br_c = _dot(ssd_ref[...], wc_ref[...])
    merged = (jax.nn.sigmoid(ga_ref[...].astype(F32)) * br_a + jax.nn.sigmoid(gb_ref[...].astype(F32)) * br_b
              + jax.nn.sigmoid(gc_ref[...].astype(F32)) * br_c)
    o_ref[...] = x_ref[...] + _dot(merged.astype(BF16), wo_ref[...])


def _merge(attn, ys5, ssd, big, x, wa, wga, wgb, wc, wo, *, tm=256):
    n = x.shape[0]

    def rows(width, col=0):
        return pl.BlockSpec((tm, width), lambda i: (i, col))

    def full(a):
        return pl.BlockSpec(a.shape, lambda i: (0, 0))

    return pl.pallas_call(
        _merge_kernel,
        grid=(n // tm,),
        in_specs=[rows(ATTN_WIDTH), rows(S5_WIDTH), rows(SSD_INNER),
                  rows(D_MODEL, 2), rows(D_MODEL, 3), rows(D_MODEL, 4), rows(D_MODEL),
                  full(wa), full(wga), full(wgb), full(wc), full(wo)],
        out_specs=rows(D_MODEL),
        out_shape=jax.ShapeDtypeStruct((n, D_MODEL), F32),
        compiler_params=_params("parallel"),
        name="merge",
    )(attn, ys5, ssd, big, big, big, x, wa, wga, wgb, wc, wo)


def _xattn_kernel(x_ref, g_ref, kv_ref, wq_ref, wo_ref, o_ref):
    x = x_ref[...]
    h = (x * lax.rsqrt(jnp.mean(x * x, axis=-1, keepdims=True) + EPS) * g_ref[...]).astype(BF16)
    q = (_dot(h, wq_ref[...]) * (XA_HEAD_DIM ** -0.5)).astype(BF16)
    kv = kv_ref[...]
    k = kv[:, :XA_WIDTH]
    v = kv[:, XA_WIDTH:]
    lane = lax.broadcasted_iota(jnp.int32, k.shape, 1)
    o = None
    for hd in range(XA_HEADS):
        sel = (lane >= hd * XA_HEAD_DIM) & (lane < (hd + 1) * XA_HEAD_DIM)
        kh = jnp.where(sel, k, 0.0).astype(BF16)
        vh = jnp.where(sel, v, 0.0).astype(BF16)
        s = _dot_nt(q, kh)
        s = s - jnp.max(s, axis=-1, keepdims=True)
        p = jnp.exp(s)
        p = p / jnp.sum(p, axis=-1, keepdims=True)
        part = _dot(p.astype(BF16), vh)
        o = part if o is None else o + part
    o_ref[...] = x + _dot(o.astype(BF16), wo_ref[...])


def _xattn(x, g, kv, wq, wo, *, batch, seq, n_mem, tm=256):
    n = batch * seq
    nt = seq // tm
    return pl.pallas_call(
        _xattn_kernel,
        grid=(batch, nt),
        in_specs=[
            pl.BlockSpec((tm, D_MODEL), lambda b, i: (b * nt + i, 0)),
            pl.BlockSpec((1, D_MODEL), lambda b, i: (0, 0)),
            pl.BlockSpec((n_mem, 2 * XA_WIDTH), lambda b, i: (b, 0)),
            pl.BlockSpec((D_MODEL, XA_WIDTH), lambda b, i: (0, 0)),
            pl.BlockSpec((XA_WIDTH, D_MODEL), lambda b, i: (0, 0)),
        ],
        out_specs=pl.BlockSpec((tm, D_MODEL), lambda b, i: (b * nt + i, 0)),
        out_shape=jax.ShapeDtypeStruct((n, D_MODEL), F32),
        compiler_params=_params("parallel", "parallel"),
        name="xattn",
    )(x, g.reshape(1, D_MODEL), kv, wq, wo)


_PEER_CELLS = [(i, j) for i in range(PEER_TOPK) for j in range(PEER_TOPK) if (i + 1) * (j + 1) <= PEER_TOPK]


def _peer_route_kernel(q_ref, k1_ref, k2_ref, e1_ref, e2_ref, thr_ref, rows_ref, kept_ref):
    tm = q_ref.shape[0]
    keys = (k1_ref[...], k2_ref[...])
    kidx = lax.broadcasted_iota(jnp.int32, (PEER_N_KEYS, tm), 0).astype(F32)

    def store(which, hd, st, rows, vals):
        for r in range(PEER_TOPK):
            rows_ref[which, r, hd:hd + 1, :] = rows[r] - rows[0]
        kept_ref[which, hd] = jnp.where(vals == NEG_INF, st - rows[0], NEG_INF)

    def scores(hd, which):
        c0 = hd * PEER_KEY_DIM + which * PEER_HALF
        return _dot_nt(keys[which], q_ref[:, c0:c0 + PEER_HALF])

    tied = jnp.zeros((1, tm), F32)
    for hd in range(PEER_HEADS):
        for which in range(2):
            st = scores(hd, which)
            vals = st
            rows = []
            for _ in range(PEER_TOPK):
                m = jnp.max(vals, axis=0, keepdims=True)
                vals = jnp.where(vals == m, NEG_INF, vals)
                rows.append(m)
            store(which, hd, st, rows, vals)
            removed = jnp.sum(jnp.where(vals == NEG_INF, 1.0, 0.0), axis=0, keepdims=True)
            tied = jnp.maximum(tied, jnp.abs(removed - float(PEER_TOPK)))

    @pl.when(jnp.max(tied) > 0.0)
    def _():
        for hd in range(PEER_HEADS):
            for which in range(2):
                st = scores(hd, which)
                vals = st
                rows = []
                for _ in range(PEER_TOPK):
                    m = jnp.max(vals, axis=0, keepdims=True)
                    first = jnp.min(jnp.where(vals == m, kidx, float(PEER_N_KEYS)), axis=0, keepdims=True)
                    vals = jnp.where(kidx == first, NEG_INF, vals)
                    rows.append(m)
                store(which, hd, st, rows, vals)

    v1 = [rows_ref[0, r] for r in range(PEER_TOPK)]
    v2 = [rows_ref[1, r] for r in range(PEER_TOPK)]
    cands = [v1[i] + v2[j] for (i, j) in _PEER_CELLS]
    work = cands
    cum = jnp.zeros_like(v1[0])
    tau = jnp.full_like(v1[0], NEG_INF)
    zsum = jnp.zeros_like(v1[0])
    for _ in range(PEER_TOPK):
        m = functools.reduce(jnp.maximum, work)
        cnt = jnp.zeros_like(m)
        nxt = []
        for c in work:
            eq = c == m
            cnt = cnt + jnp.where(eq, 1.0, 0.0)
            nxt.append(jnp.where(eq, NEG_INF, c))
        work = nxt
        open_ = cum < PEER_TOPK
        used = jnp.minimum(cnt, PEER_TOPK - cum)
        zsum = zsum + jnp.where(open_, used * jnp.exp(m), 0.0)
        tau = jnp.where(open_, m, tau)
        cum = cum + cnt
    log_z = jnp.log(zsum)
    e1v = [jnp.exp(v - log_z) for v in v1]
    e2v = [jnp.exp(v) for v in v2]
    thr = jnp.full_like(tau, jnp.inf)
    for (i, j), c in zip(_PEER_CELLS, cands):
        thr = jnp.minimum(thr, jnp.where(c >= tau, e1v[i] * e2v[j], jnp.inf))
    thr_ref[0] = thr
    for hd in range(PEER_HEADS):
        e1_ref[hd, 0] = jnp.exp(kept_ref[0, hd] - log_z[hd:hd + 1, :])
        e2_ref[hd, 0] = jnp.exp(kept_ref[1, hd])


def _peer_route(q, k1, k2):
    n = q.shape[0]
    tm = LANES
    ns = n // tm
    big_shape = jax.ShapeDtypeStruct((PEER_HEADS, ns, PEER_N_KEYS, tm), F32)
    big_spec = pl.BlockSpec((PEER_HEADS, 1, PEER_N_KEYS, tm), lambda i: (0, i, 0, 0))
    return pl.pallas_call(
        _peer_route_kernel,
        grid=(ns,),
        in_specs=[
            pl.BlockSpec((tm, PEER_HEADS * PEER_KEY_DIM), lambda i: (i, 0)),
            pl.BlockSpec((PEER_N_KEYS, PEER_HALF), lambda i: (0, 0)),
            pl.BlockSpec((PEER_N_KEYS, PEER_HALF), lambda i: (0, 0)),
        ],
        out_specs=[big_spec, big_spec, pl.BlockSpec((1, PEER_HEADS, tm), lambda i: (i, 0, 0))],
        out_shape=[big_shape, big_shape, jax.ShapeDtypeStruct((ns, PEER_HEADS, tm), F32)],
        scratch_shapes=[
            pltpu.VMEM((2, PEER_TOPK, PEER_HEADS, tm), F32),
            pltpu.VMEM((2, PEER_HEADS, PEER_N_KEYS, tm), F32),
        ],
        compiler_params=_params("parallel"),
        name="peer_route",
    )(q, k1, k2)


PEER_TE = 1024
PEER_TM = 1024
PEER_STRIP = 256


def _peer_dense_kernel(x_ref, ht_ref, e1_ref, e2_ref, thr_ref, u_ref, vt_ref, o_ref, acc_ref, act_ref, p_ref):
    j = pl.program_id(1)
    n_slab = PEER_TE // PEER_N_KEYS

    @pl.when(j == 0)
    def _():
        acc_ref[...] = jnp.zeros_like(acc_ref)

    u = u_ref[...]
    vt = vt_ref[...]
    wide = PEER_STRIP
    n_strip = PEER_TM // wide

    def first_matmul(s):
        return _dot(u, ht_ref[s])

    act_ref[0] = first_matmul(0)
    for s in range(n_strip):
        slot = s % 2
        if s + 1 < n_strip:
            act_ref[1 - slot] = first_matmul(s + 1)
        if s >= 1:
            acc_ref[s - 1] += _dot(vt, p_ref[1 - slot])
        for half in range(wide // LANES):
            tc = s * (wide // LANES) + half
            thr = thr_ref[tc]
            for al in range(n_slab):
                a = j * n_slab + al
                w = None
                for hd in range(PEER_HEADS):
                    prod = e1_ref[hd, tc, pl.ds(a, 1), :] * e2_ref[hd, tc]
                    contrib = jnp.where(prod >= thr[hd:hd + 1, :], prod, 0.0)
                    w = contrib if w is None else w + contrib
                rows = slice(al * PEER_N_KEYS, (al + 1) * PEER_N_KEYS)
                lanes = slice(half * LANES, (half + 1) * LANES)
                g = jax.nn.gelu(act_ref[slot, rows, lanes])
                p_ref[slot, rows, lanes] = (w * g).astype(BF16)
    acc_ref[n_strip - 1] += _dot(vt, p_ref[(n_strip - 1) % 2])

    @pl.when(j == pl.num_programs(1) - 1)
    def _():
        for s in range(n_strip):
            o_ref[s * wide:(s + 1) * wide, :] = x_ref[s * wide:(s + 1) * wide, :] + acc_ref[s].T


def _peer_dense(x, ht, e1, e2, thr, u, vt):
    n = x.shape[0]
    tm, te = PEER_TM, PEER_TE
    ns = tm // LANES
    big_spec = pl.BlockSpec((PEER_HEADS, ns, PEER_N_KEYS, LANES), lambda i, j: (0, i, 0, 0))
    return pl.pallas_call(
        _peer_dense_kernel,
        grid=(n // tm, PEER_N_EXPERTS // te),
        in_specs=[
            pl.BlockSpec((tm, D_MODEL), lambda i, j: (i, 0)),
            pl.BlockSpec((tm // PEER_STRIP, D_MODEL, PEER_STRIP), lambda i, j: (i, 0, 0)),
            big_spec, big_spec,
            pl.BlockSpec((ns, PEER_HEADS, LANES), lambda i, j: (i, 0, 0)),
            pl.BlockSpec((te, D_MODEL), lambda i, j: (j, 0)),
            pl.BlockSpec((D_MODEL, te), lambda i, j: (0, j)),
        ],
        out_specs=pl.BlockSpec((tm, D_MODEL), lambda i, j: (i, 0)),
        out_shape=jax.ShapeDtypeStruct((n, D_MODEL), F32),
        scratch_shapes=[
            pltpu.VMEM((tm // PEER_STRIP, D_MODEL, PEER_STRIP), F32),
            pltpu.VMEM((2, te, PEER_STRIP), F32),
            pltpu.VMEM((2, te, PEER_STRIP), BF16),
        ],
        compiler_params=_params("parallel", "arbitrary"),
        name="peer_dense",
    )(x, ht, e1, e2, thr, u, vt)


def _final_norm_kernel(x_ref, g_ref, o_ref):
    x = x_ref[...]
    o_ref[...] = x * lax.rsqrt(jnp.mean(x * x, axis=-1, keepdims=True) + EPS) * g_ref[...]


def _final_norm(x, g, *, tm=512):
    n, d = x.shape
    return pl.pallas_call(
        _final_norm_kernel,
        grid=(n // tm,),
        in_specs=[pl.BlockSpec((tm, d), lambda i: (i, 0)), pl.BlockSpec((1, d), lambda i: (0, 0))],
        out_specs=pl.BlockSpec((tm, d), lambda i: (i, 0)),
        out_shape=jax.ShapeDtypeStruct((n, d), F32),
        compiler_params=_params("parallel"),
        name="final_norm",
    )(x, g.reshape(1, d))


def _q_head_perm():
    cols = []
    for j in range(ATTN_HEADS // 2):
        for hd in (j, j + ATTN_HEADS // 2):
            cols.extend(range(hd * ATTN_HEAD_DIM, (hd + 1) * ATTN_HEAD_DIM))
    return np.asarray(cols, dtype=np.int32)


def _split_w_in(w):
    o = 0
    q = w[:, o:o + 512]; o += 512
    k = w[:, o:o + 128]; o += 128
    v = w[:, o:o + 128]; o += 128
    u = w[:, o:o + 512]; o += 512
    z = w[:, o:o + 1024]; o += 1024
    xs = w[:, o:o + 1024]; o += 1024
    bm = w[:, o:o + 128]; o += 128
    cm = w[:, o:o + 128]; o += 128
    dt = w[:, o:o + 16]; o += 16
    gates = w[:, o:o + 3072]
    q = q[:, _q_head_perm()]
    big = jnp.concatenate([z, xs, gates, q, u], axis=1).astype(BF16)
    small = jnp.concatenate([k, v, bm, cm, dt, jnp.zeros((w.shape[0], LANES - SSD_HEADS), w.dtype)], axis=1)
    return big, small.astype(BF16)


def kernel(x, mem, norm_mix, w_in, attn_sinks, s5_lambda_re, s5_lambda_im, s5_log_dt, s5_b_re, s5_b_im, s5_c_re, s5_c_im, s5_d, s5_glu_a, s5_glu_b, ssd_conv_w, ssd_conv_b, ssd_dt_bias, ssd_a_log, ssd_d, ssd_norm, w_attn_out, w_ssd_out, w_o, norm_xattn, norm_mem, xa_wq, xa_wkv, xa_wo, norm_ffn, peer_wq, peer_k1, peer_k2, peer_u, peer_v, norm_final):
    batch, seq, d = x.shape
    n = batch * seq
    n_mem = mem.shape[1]
    depth = w_in.shape[0]
    xf = x.reshape(n, d).astype(F32)
    memf = mem.reshape(batch * n_mem, d).astype(F32)
    for i in range(depth):
        w_big, w_small = _split_w_in(w_in[i])
        big = _norm_mm(xf, norm_mix[i], w_big, tm=1024, tn=512, out_dtype=BF16, name="in_proj_big")
        small = _norm_mm(xf, norm_mix[i], w_small, tm=1024, tn=SMALL_COLS, out_dtype=F32, name="in_proj_small")
        attn = _swa(big, small, attn_sinks[i].astype(F32), batch=batch, seq=seq)
        wb, wc, sc = _s5_prepare(s5_lambda_re[i], s5_lambda_im[i], s5_log_dt[i], s5_b_re[i], s5_b_im[i],
                                 s5_c_re[i], s5_c_im[i])
        ys5 = _s5(big, wb, wc, s5_d[i].astype(F32), sc, batch=batch, seq=seq)
        ssd = _ssd(big, small, ssd_conv_w[i], ssd_conv_b[i], ssd_dt_bias[i], ssd_a_log[i], ssd_d[i], ssd_norm[i],
                   batch=batch, seq=seq)
        xf = _merge(attn, ys5, ssd, big, xf,
                    w_attn_out[i][_q_head_perm(), :].astype(BF16), s5_glu_a[i].astype(BF16),
                    s5_glu_b[i].astype(BF16), w_ssd_out[i].astype(BF16), w_o[i].astype(BF16))
        kv = _norm_mm(memf, norm_mem[i], xa_wkv[i].astype(BF16), tm=batch * n_mem, tn=2 * XA_WIDTH,
                      out_dtype=F32, name="mem_kv")
        xf = _xattn(xf, norm_xattn[i], kv, xa_wq[i].astype(BF16), xa_wo[i].astype(BF16),
                    batch=batch, seq=seq, n_mem=n_mem)
        q, ht = _norm_mm(xf, norm_ffn[i], peer_wq[i].astype(BF16), tm=1024, tn=512, out_dtype=BF16,
                        with_h=True, name="peer_query")
        e1, e2, thr = _peer_route(q, peer_k1[i].astype(BF16), peer_k2[i].astype(BF16))
        xf = _peer_dense(xf, ht, e1, e2, thr, peer_u[i].astype(BF16), peer_v[i].astype(BF16).T)
    return _final_norm(xf, norm_final.astype(F32)).reshape(batch, seq, d)
```

```python
import functools
import math

import jax
import jax.numpy as jnp
import numpy as np
from jax import lax
from jax.experimental import pallas as pl
from jax.experimental.pallas import tpu as pltpu

F32 = jnp.float32
BF16 = jnp.bfloat16
NEG_INF = float("-inf")

D_MODEL = 1024
EPS = 1e-6
ATTN_HEADS = 8
ATTN_HEAD_DIM = 64
ATTN_WIDTH = ATTN_HEADS * ATTN_HEAD_DIM
ATTN_BLOCK = 128
S5_WIDTH = 512
S5_GROUP = 16
S5_GROUPS = S5_WIDTH // S5_GROUP
S5_STATE = 64
S5_LANES = S5_GROUPS * S5_STATE
S5_CHUNK = 256
S5_SCAN_WIDTH = 512
SSD_INNER = 1024
SSD_HEAD_DIM = 64
SSD_HEADS = SSD_INNER // SSD_HEAD_DIM
SSD_STATE = 64
SSD_GROUPS = 2
SSD_CONV = 4
SSD_CHUNK = 128
SSD_BC = SSD_GROUPS * SSD_STATE
SSD_CONV_DIM = SSD_INNER + 2 * SSD_BC
XA_HEADS = 4
XA_HEAD_DIM = 64
XA_WIDTH = XA_HEADS * XA_HEAD_DIM
PEER_HEADS = 8
PEER_KEY_DIM = 256
PEER_HALF = PEER_KEY_DIM // 2
PEER_N_KEYS = 128
PEER_TOPK = 16
PEER_N_EXPERTS = PEER_N_KEYS * PEER_N_KEYS
LANES = 128
SUBLANES = 8
VMEM_LIMIT_BYTES = 56 * 1024 * 1024

BIG_COLS = 6144
SMALL_COLS = 640


def _params(*sem):
    return pltpu.CompilerParams(dimension_semantics=sem, vmem_limit_bytes=VMEM_LIMIT_BYTES)


def _dot(a, b):
    return jnp.dot(a, b, preferred_element_type=F32)


def _dot_nt(a, b):
    return lax.dot_general(a, b, (((1,), (1,)), ((), ())), preferred_element_type=F32)


def _split3(x):
    p1 = x.astype(BF16)
    r1 = x - p1.astype(F32)
    p2 = r1.astype(BF16)
    p3 = (r1 - p2.astype(F32)).astype(BF16)
    return p1, p2, p3


def _exact_dot_rhs(sel, x):
    p1, p2, p3 = _split3(x)
    return _dot(sel, p1) + _dot(sel, p2) + _dot(sel, p3)


def _exact_dot_lhs(x, sel):
    p1, p2, p3 = _split3(x)
    return _dot(p1, sel) + _dot(p2, sel) + _dot(p3, sel)


def _norm_mm_kernel(x_ref, g_ref, w_ref, o_ref, *rest, with_h):
    if with_h:
        hout_ref, h_ref = rest
    else:
        (h_ref,) = rest

    @pl.when(pl.program_id(1) == 0)
    def _():
        x = x_ref[...]
        h = x * lax.rsqrt(jnp.mean(x * x, axis=-1, keepdims=True) + EPS) * g_ref[...]
        h_ref[...] = h.astype(BF16)
        if with_h:
            for s in range(hout_ref.shape[0]):
                hout_ref[s] = h[s * PEER_STRIP:(s + 1) * PEER_STRIP, :].T.astype(BF16)

    o_ref[...] = _dot(h_ref[...], w_ref[...]).astype(o_ref.dtype)


def _norm_mm(x, g, w, *, tm, tn, out_dtype, with_h=False, name):
    n, k = x.shape
    m = w.shape[1]
    assert n % tm == 0 and m % tn == 0
    out_shape = [jax.ShapeDtypeStruct((n, m), out_dtype)]
    out_specs = [pl.BlockSpec((tm, tn), lambda i, j: (i, j))]
    if with_h:
        out_shape.append(jax.ShapeDtypeStruct((n // PEER_STRIP, k, PEER_STRIP), BF16))
        out_specs.append(pl.BlockSpec((tm // PEER_STRIP, k, PEER_STRIP), lambda i, j: (i, 0, 0)))
    res = pl.pallas_call(
        functools.partial(_norm_mm_kernel, with_h=with_h),
        grid=(n // tm, m // tn),
        in_specs=[
            pl.BlockSpec((tm, k), lambda i, j: (i, 0)),
            pl.BlockSpec((1, k), lambda i, j: (0, 0)),
            pl.BlockSpec((k, tn), lambda i, j: (0, j)),
        ],
        out_specs=out_specs,
        out_shape=out_shape,
        scratch_shapes=[pltpu.VMEM((tm, k), BF16)],
        compiler_params=_params("parallel", "arbitrary"),
        name=name,
    )(x, g.reshape(1, k), w)
    return res if with_h else res[0]


def _swa_kernel(sink_ref, q_ref, kc_ref, kp_ref, vc_ref, vp_ref, o_ref):
    nblk = pl.program_id(1)
    blk = ATTN_BLOCK
    q = q_ref[...]
    k = jnp.concatenate([kp_ref[...], kc_ref[...]], axis=0)
    v = jnp.concatenate([vp_ref[...], vc_ref[...]], axis=0).astype(BF16)
    lane_k = lax.broadcasted_iota(jnp.int32, (2 * blk, LANES), 1)
    k_lo = jnp.where(lane_k < ATTN_HEAD_DIM, k, 0.0).astype(BF16)
    k_hi = jnp.where(lane_k < ATTN_HEAD_DIM, 0.0, k).astype(BF16)
    qi = lax.broadcasted_iota(jnp.int32, (blk, 2 * blk), 0)
    kj = lax.broadcasted_iota(jnp.int32, (blk, 2 * blk), 1)
    dist = qi - kj + blk
    valid = (dist >= 0) & (dist < blk) & ((nblk > 0) | (kj >= blk))
    distf = dist.astype(F32)
    lane_o = lax.broadcasted_iota(jnp.int32, (blk, LANES), 1)
    outs = []
    for j in range(ATTN_HEADS // 2):
        qs = (q[:, j * LANES:(j + 1) * LANES] * (ATTN_HEAD_DIM ** -0.5)).astype(BF16)
        res = []
        for half, kk in ((0, k_lo), (1, k_hi)):
            head = j + 4 * half
            slope = 2.0 ** (-8.0 * (head + 1) / ATTN_HEADS)
            s = _dot_nt(qs, kk) - slope * distf
            s = jnp.where(valid, s, NEG_INF)
            sink = sink_ref[head]
            m = jnp.maximum(jnp.max(s, axis=-1, keepdims=True), sink)
            p = jnp.exp(s - m)
            denom = jnp.sum(p, axis=-1, keepdims=True) + jnp.exp(sink - m)
            p = p / denom
            res.append(_dot(p.astype(BF16), v))
        outs.append(jnp.where(lane_o < ATTN_HEAD_DIM, res[0], res[1]))
    o_ref[...] = jnp.concatenate(outs, axis=1).astype(o_ref.dtype)


def _swa(big, small, sinks, *, batch, seq):
    nb = seq // ATTN_BLOCK
    n = batch * seq
    qcol = 5120 // ATTN_WIDTH

    def cur(col):
        return lambda b, i: (b * nb + i, col)

    def prev(col):
        return lambda b, i: (b * nb + jnp.maximum(i - 1, 0), col)

    return pl.pallas_call(
        _swa_kernel,
        grid=(batch, nb),
        in_specs=[
            pl.BlockSpec(memory_space=pltpu.SMEM),
            pl.BlockSpec((ATTN_BLOCK, ATTN_WIDTH), cur(qcol)),
            pl.BlockSpec((ATTN_BLOCK, LANES), cur(0)),
            pl.BlockSpec((ATTN_BLOCK, LANES), prev(0)),
            pl.BlockSpec((ATTN_BLOCK, LANES), cur(1)),
            pl.BlockSpec((ATTN_BLOCK, LANES), prev(1)),
        ],
        out_specs=pl.BlockSpec((ATTN_BLOCK, ATTN_WIDTH), lambda b, i: (b * nb + i, 0)),
        out_shape=jax.ShapeDtypeStruct((n, ATTN_WIDTH), BF16),
        compiler_params=_params("parallel", "arbitrary"),
        name="swa",
    )(sinks, big, small, small, small, small)


def _s5_kernel(u_ref, wb_ref, wc_ref, d_ref, sc_ref, o_ref, xr_ref, xi_ref, cr_ref, ci_ref):
    @pl.when(pl.program_id(1) == 0)
    def _():
        cr_ref[...] = jnp.zeros_like(cr_ref)
        ci_ref[...] = jnp.zeros_like(ci_ref)

    u = u_ref[...].astype(F32)
    bu = _dot(u_ref[...], wb_ref[...])
    xr_ref[...] = bu[:, :S5_LANES]
    xi_ref[...] = bu[:, S5_LANES:]

    def cmul_add(xr, xi, ar, ai, sr, si):
        return xr + (ar * sr - ai * si), xi + (ar * si + ai * sr)

    for w in range(S5_LANES // S5_SCAN_WIDTH):
        sl = slice(w * S5_SCAN_WIDTH, (w + 1) * S5_SCAN_WIDTH)
        consts = [sc_ref[c, :, sl] for c in range(8)]
        a1r, a1i, a2r, a2i, a4r, a4i, pr, pi = consts

        def body(t, carry):
            cr, ci = carry
            r0 = pl.multiple_of(t * SUBLANES, SUBLANES)
            xr = xr_ref[pl.ds(r0, SUBLANES), sl]
            xi = xi_ref[pl.ds(r0, SUBLANES), sl]
            xr, xi = cmul_add(xr, xi, a1r, a1i, pltpu.roll(xr, 1, 0), pltpu.roll(xi, 1, 0))
            xr, xi = cmul_add(xr, xi, a2r, a2i, pltpu.roll(xr, 2, 0), pltpu.roll(xi, 2, 0))
            xr, xi = cmul_add(xr, xi, a4r, a4i, pltpu.roll(xr, 4, 0), pltpu.roll(xi, 4, 0))
            xr, xi = cmul_add(xr, xi, pr, pi, cr, ci)
            xr_ref[pl.ds(r0, SUBLANES), sl] = xr
            xi_ref[pl.ds(r0, SUBLANES), sl] = xi
            return xr[SUBLANES - 1:SUBLANES, :], xi[SUBLANES - 1:SUBLANES, :]

        cr, ci = lax.fori_loop(0, S5_CHUNK // SUBLANES, body, (cr_ref[0:1, sl], ci_ref[0:1, sl]))
        cr_ref[0:1, sl] = cr
        ci_ref[0:1, sl] = ci

    xcat = jnp.concatenate([xr_ref[...].astype(BF16), xi_ref[...].astype(BF16)], axis=1)
    y = _dot(xcat, wc_ref[...]) + d_ref[...] * u
    o_ref[...] = jax.nn.gelu(y).astype(o_ref.dtype)


def _s5_prepare(lam_re, lam_im, log_dt, b_re, b_im, c_re, c_im):
    dt = jnp.exp(log_dt.astype(F32))[:, None]
    lr = lam_re.astype(F32)
    li = lam_im.astype(F32)
    mag = jnp.exp(lr * dt)
    ar = mag * jnp.cos(li * dt)
    ai = mag * jnp.sin(li * dt)
    den = lr * lr + li * li
    nr = ar - 1.0
    wr = (nr * lr + ai * li) / den
    wi = (ai * lr - nr * li) / den
    br_, bi_ = b_re.astype(F32), b_im.astype(F32)
    bbr = wr[..., None] * br_ - wi[..., None] * bi_
    bbi = wr[..., None] * bi_ + wi[..., None] * br_
    eye = jnp.eye(S5_GROUPS, dtype=F32)
    wbr = jnp.einsum('gpi,gh->gihp', bbr, eye).reshape(S5_WIDTH, S5_LANES)
    wbi = jnp.einsum('gpi,gh->gihp', bbi, eye).reshape(S5_WIDTH, S5_LANES)
    wb = jnp.concatenate([wbr, wbi], axis=1).astype(BF16)
    wcr = jnp.einsum('gip,gh->gphi', c_re.astype(F32), eye).reshape(S5_LANES, S5_WIDTH)
    wci = jnp.einsum('gip,gh->gphi', c_im.astype(F32), eye).reshape(S5_LANES, S5_WIDTH)
    wc = jnp.concatenate([wcr, -wci], axis=0).astype(BF16)
    ar_f = ar.reshape(1, S5_LANES)
    ai_f = ai.reshape(1, S5_LANES)

    def cmul(xr, xi, yr, yi):
        return xr * yr - xi * yi, xr * yi + xi * yr

    pows_r, pows_i = [ar_f], [ai_f]
    for _ in range(SUBLANES - 1):
        nr_, ni_ = cmul(pows_r[-1], pows_i[-1], ar_f, ai_f)
        pows_r.append(nr_)
        pows_i.append(ni_)
    t_idx = jnp.arange(SUBLANES)[:, None]

    def masked(k):
        return (jnp.where(t_idx >= k, pows_r[k - 1], 0.0), jnp.where(t_idx >= k, pows_i[k - 1], 0.0))

    a1 = masked(1)
    a2 = masked(2)
    a4 = masked(4)
    pr = jnp.concatenate(pows_r, axis=0)
    pi = jnp.concatenate(pows_i, axis=0)
    sc = jnp.stack([a1[0], a1[1], a2[0], a2[1], a4[0], a4[1], pr, pi], axis=0)
    return wb, wc, sc


def _s5(big, wb, wc, d, sc, *, batch, seq):
    n = batch * seq
    nc = seq // S5_CHUNK
    ucol = 5632 // S5_WIDTH
    return pl.pallas_call(
        _s5_kernel,
        grid=(batch, nc),
        in_specs=[
            pl.BlockSpec((S5_CHUNK, S5_WIDTH), lambda b, c: (b * nc + c, ucol)),
            pl.BlockSpec((S5_WIDTH, 2 * S5_LANES), lambda b, c: (0, 0)),
            pl.BlockSpec((2 * S5_LANES, S5_WIDTH), lambda b, c: (0, 0)),
            pl.BlockSpec((1, S5_WIDTH), lambda b, c: (0, 0)),
            pl.BlockSpec((8, SUBLANES, S5_LANES), lambda b, c: (0, 0, 0)),
        ],
        out_specs=pl.BlockSpec((S5_CHUNK, S5_WIDTH), lambda b, c: (b * nc + c, 0)),
        out_shape=jax.ShapeDtypeStruct((n, S5_WIDTH), BF16),
        scratch_shapes=[
            pltpu.VMEM((S5_CHUNK, S5_LANES), F32),
            pltpu.VMEM((S5_CHUNK, S5_LANES), F32),
            pltpu.VMEM((SUBLANES, S5_LANES), F32),
            pltpu.VMEM((SUBLANES, S5_LANES), F32),
        ],
        compiler_params=_params("parallel", "arbitrary"),
        name="s5",
    )(big, wb, wc, d.reshape(1, S5_WIDTH), sc)


def _ssd_kernel(z_ref, xs_ref, b_ref, c_ref, dt_ref, cw_ref, cb_ref, dtb_ref, alog_ref, dvec_ref, nw_ref,
                exp_h_ref, exp_l_ref, o_ref, buf_ref, st_ref):
    q = SSD_CHUNK

    @pl.when(pl.program_id(1) == 0)
    def _():
        buf_ref[0:SUBLANES, :] = jnp.zeros((SUBLANES, SSD_CONV_DIM), F32)
        st_ref[...] = jnp.zeros_like(st_ref)

    buf_ref[SUBLANES:SUBLANES + q, 0:SSD_INNER] = xs_ref[...].astype(F32)
    buf_ref[SUBLANES:SUBLANES + q, SSD_INNER:SSD_INNER + SSD_BC] = b_ref[...]
    buf_ref[SUBLANES:SUBLANES + q, SSD_INNER + SSD_BC:SSD_CONV_DIM] = c_ref[...]
    acc = jnp.broadcast_to(cb_ref[...], (q, SSD_CONV_DIM))
    for k in range(SSD_CONV):
        acc = acc + cw_ref[k:k + 1, :] * buf_ref[pl.ds(SUBLANES - (SSD_CONV - 1) + k, q), :]
    buf_ref[0:SUBLANES, :] = buf_ref[q:q + SUBLANES, :]
    xbc = acc * jax.nn.sigmoid(acc)
    xs = xbc[:, :SSD_INNER]
    bm = xbc[:, SSD_INNER:SSD_INNER + SSD_BC]
    cm = xbc[:, SSD_INNER + SSD_BC:]

    lane = lax.broadcasted_iota(jnp.int32, (q, LANES), 1)
    row = lax.broadcasted_iota(jnp.int32, (q, LANES), 0)
    head_lane = lane < SSD_HEADS
    dt_in = dt_ref[...] + dtb_ref[...]
    dt = jnp.maximum(dt_in, 0.0) + jnp.log1p(jnp.exp(-jnp.abs(dt_in)))
    dt = jnp.where(head_lane, dt, 0.0)
    a = -jnp.exp(alog_ref[...])
    ad = dt * a
    tri = (row >= lane).astype(BF16)
    cs = _exact_dot_rhs(tri, ad)
    cs_t = cs.T
    dt_t = dt.T
    total = cs[q - 1:q, :]
    dec_t = jnp.exp(cs_t[:, q - 1:q] - cs_t)
    w_t = dec_t * dt_t

    exp_h = exp_h_ref[...]
    exp_l = exp_l_ref[...]
    ecs_x = _exact_dot_lhs(jnp.exp(cs), exp_h)
    w_x = _exact_dot_lhs(w_t.T, exp_h)
    cs_cols = _exact_dot_lhs(cs, exp_l)
    tot_x = _exact_dot_lhs(jnp.broadcast_to(jnp.exp(total), (SUBLANES, LANES)), exp_h)[0:1, :]

    bm16 = bm.astype(BF16)
    cm16 = cm.astype(BF16)
    xs16 = xs.astype(BF16)
    low = lane < SSD_STATE
    g_mats = [_dot_nt(jnp.where(low, cm, 0.0).astype(BF16), bm16),
              _dot_nt(jnp.where(low, 0.0, cm).astype(BF16), bm16)]
    causal = row >= lane
    ydiag = []
    for j in range(SSD_HEADS // 2):
        slab = xs[:, j * LANES:(j + 1) * LANES]
        halves = (jnp.where(low, slab, 0.0).astype(BF16), jnp.where(low, 0.0, slab).astype(BF16))
        acc_j = None
        for half in range(2):
            h = 2 * j + half
            g = h // (SSD_HEADS // SSD_GROUPS)
            seg = cs_cols[:, h * LANES:(h + 1) * LANES] - cs_t[h:h + 1, :]
            lmat = jnp.exp(jnp.where(causal, seg, NEG_INF))
            m = (g_mats[g] * lmat * dt_t[h:h + 1, :]).astype(BF16)
            part = _dot(m, halves[half])
            acc_j = part if acc_j is None else acc_j + part
        ydiag.append(acc_j)
    y = jnp.concatenate(ydiag, axis=1)

    st = st_ref[...]
    y = y + _dot(cm16, st.astype(BF16)) * ecs_x
    new = _dot(bm.T.astype(BF16), (xs * w_x).astype(BF16))
    srow = lax.broadcasted_iota(jnp.int32, (LANES, SSD_INNER), 0)
    scol = lax.broadcasted_iota(jnp.int32, (LANES, SSD_INNER), 1)
    same_group = (srow < SSD_STATE) == (scol < SSD_INNER // SSD_GROUPS)
    st_ref[...] = jnp.where(same_group, st * tot_x + new, 0.0)

    y = y + dvec_ref[...] * xs
    zz = z_ref[...].astype(F32)
    gated = y * (zz * jax.nn.sigmoid(zz))
    out = gated * lax.rsqrt(jnp.mean(gated * gated, axis=-1, keepdims=True) + EPS) * nw_ref[...]
    o_ref[...] = out.astype(o_ref.dtype)


def _ssd(big, small, conv_w, conv_b, dt_bias, a_log, dvec, norm_w, *, batch, seq):
    n = batch * seq
    nc = seq // SSD_CHUNK
    q = SSD_CHUNK
    pad = LANES - SSD_HEADS
    dtb = jnp.pad(dt_bias.astype(F32), (0, pad)).reshape(1, LANES)
    alog = jnp.pad(a_log.astype(F32), (0, pad)).reshape(1, LANES)
    d_x = jnp.repeat(dvec.astype(F32), SSD_HEAD_DIM).reshape(1, SSD_INNER)
    heads = np.arange(LANES)[:, None]
    exp_h = jnp.asarray(heads == (np.arange(SSD_INNER)[None, :] // SSD_HEAD_DIM), dtype=BF16)
    exp_l = jnp.asarray(heads == (np.arange(SSD_HEADS * LANES)[None, :] // LANES), dtype=BF16)

    def blk(col, width):
        return pl.BlockSpec((q, width), lambda b, c: (b * nc + c, col))

    def full(shape):
        return pl.BlockSpec(shape, lambda b, c: (0,) * len(shape))

    return pl.pallas_call(
        _ssd_kernel,
        grid=(batch, nc),
        in_specs=[
            blk(0, SSD_INNER), blk(1, SSD_INNER),
            blk(2, LANES), blk(3, LANES), blk(4, LANES),
            full((SSD_CONV, SSD_CONV_DIM)), full((1, SSD_CONV_DIM)),
            full((1, LANES)), full((1, LANES)), full((1, SSD_INNER)), full((1, SSD_INNER)),
            full((LANES, SSD_INNER)), full((LANES, SSD_HEADS * LANES)),
        ],
        out_specs=pl.BlockSpec((q, SSD_INNER), lambda b, c: (b * nc + c, 0)),
        out_shape=jax.ShapeDtypeStruct((n, SSD_INNER), BF16),
        scratch_shapes=[
            pltpu.VMEM((q + SUBLANES, SSD_CONV_DIM), F32),
            pltpu.VMEM((LANES, SSD_INNER), F32),
        ],
        compiler_params=_params("parallel", "arbitrary"),
        name="ssd",
    )(big, big, small, small, small, conv_w.astype(F32), conv_b.astype(F32).reshape(1, SSD_CONV_DIM),
      dtb, alog, d_x, norm_w.astype(F32).reshape(1, SSD_INNER), exp_h, exp_l)


def _merge_kernel(attn_ref, ys5_ref, ssd_ref, ga_ref, gb_ref, gc_ref, x_ref,
                  wa_ref, wga_ref, wgb_ref, wc_ref, wo_ref, o_ref):
    ys5 = ys5_ref[...]
    br_a = _dot(attn_ref[...], wa_ref[...])
    br_b = _dot(ys5, wga_ref[...]) * jax.nn.sigmoid(_dot(ys5, wgb_ref[...]))
    br_c = _dot(ssd_ref[...], wc_ref[...])
    merged = (jax.nn.sigmoid(ga_ref[...].astype(F32)) * br_a + jax.nn.sigmoid(gb_ref[...].astype(F32)) * br_b
              + jax.nn.sigmoid(gc_ref[...].astype(F32)) * br_c)
    o_ref[...] = x_ref[...] + _dot(merged.astype(BF16), wo_ref[...])


def _merge(attn, ys5, ssd, big, x, wa, wga, wgb, wc, wo, *, tm=256):
    n = x.shape[0]

    def rows(width, col=0):
        return pl.BlockSpec((tm, width), lambda i: (i, col))

    def full(a):
        return pl.BlockSpec(a.shape, lambda i: (0, 0))

    return pl.pallas_call(
        _merge_kernel,
        grid=(n // tm,),
        in_specs=[rows(ATTN_WIDTH), rows(S5_WIDTH), rows(SSD_INNER),
                  rows(D_MODEL, 2), rows(D_MODEL, 3), rows(D_MODEL, 4), rows(D_MODEL),
                  full(wa), full(wga), full(wgb), full(wc), full(wo)],
        out_specs=rows(D_MODEL),
        out_shape=jax.ShapeDtypeStruct((n, D_MODEL), F32),
        compiler_params=_params("parallel"),
        name="merge",
    )(attn, ys5, ssd, big, big, big, x, wa, wga, wgb, wc, wo)


def _xattn_kernel(x_ref, g_ref, kv_ref, wq_ref, wo_ref, o_ref):
    x = x_ref[...]
    h = (x * lax.rsqrt(jnp.mean(x * x, axis=-1, keepdims=True) + EPS) * g_ref[...]).astype(BF16)
    q = (_dot(h, wq_ref[...]) * (XA_HEAD_DIM ** -0.5)).astype(BF16)
    kv = kv_ref[...]
    k = kv[:, :XA_WIDTH]
    v = kv[:, XA_WIDTH:]
    lane = lax.broadcasted_iota(jnp.int32, k.shape, 1)
    o = None
    for hd in range(XA_HEADS):
        sel = (lane >= hd * XA_HEAD_DIM) & (lane < (hd + 1) * XA_HEAD_DIM)
        kh = jnp.where(sel, k, 0.0).astype(BF16)
        vh = jnp.where(sel, v, 0.0).astype(BF16)
        s = _dot_nt(q, kh)
        s = s - jnp.max(s, axis=-1, keepdims=True)
        p = jnp.exp(s)
        p = p / jnp.sum(p, axis=-1, keepdims=True)
        part = _dot(p.astype(BF16), vh)
        o = part if o is None else o + part
    o_ref[...] = x + _dot(o.astype(BF16), wo_ref[...])


def _xattn(x, g, kv, wq, wo, *, batch, seq, n_mem, tm=256):
    n = batch * seq
    nt = seq // tm
    return pl.pallas_call(
        _xattn_kernel,
        grid=(batch, nt),
        in_specs=[
            pl.BlockSpec((tm, D_MODEL), lambda b, i: (b * nt + i, 0)),
            pl.BlockSpec((1, D_MODEL), lambda b, i: (0, 0)),
            pl.BlockSpec((n_mem, 2 * XA_WIDTH), lambda b, i: (b, 0)),
            pl.BlockSpec((D_MODEL, XA_WIDTH), lambda b, i: (0, 0)),
            pl.BlockSpec((XA_WIDTH, D_MODEL), lambda b, i: (0, 0)),
        ],
        out_specs=pl.BlockSpec((tm, D_MODEL), lambda b, i: (b * nt + i, 0)),
        out_shape=jax.ShapeDtypeStruct((n, D_MODEL), F32),
        compiler_params=_params("parallel", "parallel"),
        name="xattn",
    )(x, g.reshape(1, D_MODEL), kv, wq, wo)


_PEER_CELLS = [(i, j) for i in range(PEER_TOPK) for j in range(PEER_TOPK) if (i + 1) * (j + 1) <= PEER_TOPK]


def _peer_route_kernel(q_ref, k1_ref, k2_ref, e1_ref, e2_ref, thr_ref, rows_ref, kept_ref):
    tm = q_ref.shape[0]
    keys = (k1_ref[...], k2_ref[...])
    kidx = lax.broadcasted_iota(jnp.int32, (PEER_N_KEYS, tm), 0).astype(F32)

    def store(which, hd, st, rows, vals):
        for r in range(PEER_TOPK):
            rows_ref[which, r, hd:hd + 1, :] = rows[r] - rows[0]
        kept_ref[which, hd] = jnp.where(vals == NEG_INF, st - rows[0], NEG_INF)

    def scores(hd, which):
        c0 = hd * PEER_KEY_DIM + which * PEER_HALF
        return _dot_nt(keys[which], q_ref[:, c0:c0 + PEER_HALF])

    tied = jnp.zeros((1, tm), F32)
    for hd in range(PEER_HEADS):
        for which in range(2):
            st = scores(hd, which)
            vals = st
            rows = []
            for _ in range(PEER_TOPK):
                m = jnp.max(vals, axis=0, keepdims=True)
                vals = jnp.where(vals == m, NEG_INF, vals)
                rows.append(m)
            store(which, hd, st, rows, vals)
            removed = jnp.sum(jnp.where(vals == NEG_INF, 1.0, 0.0), axis=0, keepdims=True)
            tied = jnp.maximum(tied, jnp.abs(removed - float(PEER_TOPK)))

    @pl.when(jnp.max(tied) > 0.0)
    def _():
        for hd in range(PEER_HEADS):
            for which in range(2):
                st = scores(hd, which)
                vals = st
                rows = []
                for _ in range(PEER_TOPK):
                    m = jnp.max(vals, axis=0, keepdims=True)
                    first = jnp.min(jnp.where(vals == m, kidx, float(PEER_N_KEYS)), axis=0, keepdims=True)
                    vals = jnp.where(kidx == first, NEG_INF, vals)
                    rows.append(m)
                store(which, hd, st, rows, vals)

    v1 = [rows_ref[0, r] for r in range(PEER_TOPK)]
    v2 = [rows_ref[1, r] for r in range(PEER_TOPK)]
    cands = [v1[i] + v2[j] for (i, j) in _PEER_CELLS]
    work = cands
    cum = jnp.zeros_like(v1[0])
    tau = jnp.full_like(v1[0], NEG_INF)
    zsum = jnp.zeros_like(v1[0])
    for _ in range(PEER_TOPK):
        m = functools.reduce(jnp.maximum, work)
        cnt = jnp.zeros_like(m)
        nxt = []
        for c in work:
            eq = c == m
            cnt = cnt + jnp.where(eq, 1.0, 0.0)
            nxt.append(jnp.where(eq, NEG_INF, c))
        work = nxt
        open_ = cum < PEER_TOPK
        used = jnp.minimum(cnt, PEER_TOPK - cum)
        zsum = zsum + jnp.where(open_, used * jnp.exp(m), 0.0)
        tau = jnp.where(open_, m, tau)
        cum = cum + cnt
    log_z = jnp.log(zsum)
    e1v = [jnp.exp(v - log_z) for v in v1]
    e2v = [jnp.exp(v) for v in v2]
    thr = jnp.full_like(tau, jnp.inf)
    for (i, j), c in zip(_PEER_CELLS, cands):
        thr = jnp.minimum(thr, jnp.where(c >= tau, e1v[i] * e2v[j], jnp.inf))
    thr_ref[0] = thr
    for hd in range(PEER_HEADS):
        e1_ref[hd, 0] = jnp.exp(kept_ref[0, hd] - log_z[hd:hd + 1, :])
        e2_ref[hd, 0] = jnp.exp(kept_ref[1, hd])


def _peer_route(q, k1, k2):
    n = q.shape[0]
    tm = LANES
    ns = n // tm
    big_shape = jax.ShapeDtypeStruct((PEER_HEADS, ns, PEER_N_KEYS, tm), F32)
    big_spec = pl.BlockSpec((PEER_HEADS, 1, PEER_N_KEYS, tm), lambda i: (0, i, 0, 0))
    return pl.pallas_call(
        _peer_route_kernel,
        grid=(ns,),
        in_specs=[
            pl.BlockSpec((tm, PEER_HEADS * PEER_KEY_DIM), lambda i: (i, 0)),
            pl.BlockSpec((PEER_N_KEYS, PEER_HALF), lambda i: (0, 0)),
            pl.BlockSpec((PEER_N_KEYS, PEER_HALF), lambda i: (0, 0)),
        ],
        out_specs=[big_spec, big_spec, pl.BlockSpec((1, PEER_HEADS, tm), lambda i: (i, 0, 0))],
        out_shape=[big_shape, big_shape, jax.ShapeDtypeStruct((ns, PEER_HEADS, tm), F32)],
        scratch_shapes=[
            pltpu.VMEM((2, PEER_TOPK, PEER_HEADS, tm), F32),
            pltpu.VMEM((2, PEER_HEADS, PEER_N_KEYS, tm), F32),
        ],
        compiler_params=_params("parallel"),
        name="peer_route",
    )(q, k1, k2)


PEER_TE = 1024
PEER_TM = 1024
PEER_STRIP = 256


def _gelu(x):
    c = math.sqrt(2.0 / math.pi)
    inner = x * ((x * x) * (0.044715 * c) + c)
    hx = 0.5 * x
    return hx + hx * jnp.tanh(inner)


def _peer_dense_kernel(x_ref, ht_ref, e1_ref, e2_ref, thr_ref, u_ref, vt_ref, o_ref, acc_ref, act_ref, p_ref):
    j = pl.program_id(1)
    n_slab = PEER_TE // PEER_N_KEYS

    @pl.when(j == 0)
    def _():
        acc_ref[...] = jnp.zeros_like(acc_ref)

    u = u_ref[...]
    vt = vt_ref[...]
    wide = PEER_STRIP
    n_strip = PEER_TM // wide

    def first_matmul(s):
        return _dot(u, ht_ref[s])

    act_ref[0] = first_matmul(0)
    for s in range(n_strip):
        slot = s % 2
        if s + 1 < n_strip:
            act_ref[1 - slot] = first_matmul(s + 1)
        if s >= 1:
            acc_ref[s - 1] += _dot(vt, p_ref[1 - slot])
        for half in range(wide // LANES):
            tc = s * (wide // LANES) + half
            thr = thr_ref[tc]
            for al in range(n_slab):
                a = j * n_slab + al
                w = None
                for hd in range(PEER_HEADS):
                    prod = e1_ref[hd, tc, pl.ds(a, 1), :] * e2_ref[hd, tc]
                    contrib = jnp.where(prod >= thr[hd:hd + 1, :], prod, 0.0)
                    w = contrib if w is None else w + contrib
                rows = slice(al * PEER_N_KEYS, (al + 1) * PEER_N_KEYS)
                lanes = slice(half * LANES, (half + 1) * LANES)
                g = _gelu(act_ref[slot, rows, lanes])
                p_ref[slot, rows, lanes] = (w * g).astype(BF16)
    acc_ref[n_strip - 1] += _dot(vt, p_ref[(n_strip - 1) % 2])

    @pl.when(j == pl.num_programs(1) - 1)
    def _():
        for s in range(n_strip):
            o_ref[s * wide:(s + 1) * wide, :] = x_ref[s * wide:(s + 1) * wide, :] + acc_ref[s].T


def _peer_dense(x, ht, e1, e2, thr, u, vt):
    n = x.shape[0]
    tm, te = PEER_TM, PEER_TE
    ns = tm // LANES
    big_spec = pl.BlockSpec((PEER_HEADS, ns, PEER_N_KEYS, LANES), lambda i, j: (0, i, 0, 0))
    return pl.pallas_call(
        _peer_dense_kernel,
        grid=(n // tm, PEER_N_EXPERTS // te),
        in_specs=[
            pl.BlockSpec((tm, D_MODEL), lambda i, j: (i, 0)),
            pl.BlockSpec((tm // PEER_STRIP, D_MODEL, PEER_STRIP), lambda i, j: (i, 0, 0)),
            big_spec, big_spec,
            pl.BlockSpec((ns, PEER_HEADS, LANES), lambda i, j: (i, 0, 0)),
            pl.BlockSpec((te, D_MODEL), lambda i, j: (j, 0)),
            pl.BlockSpec((D_MODEL, te), lambda i, j: (0, j)),
        ],
        out_specs=pl.BlockSpec((tm, D_MODEL), lambda i, j: (i, 0)),
        out_shape=jax.ShapeDtypeStruct((n, D_MODEL), F32),
        scratch_shapes=[
            pltpu.VMEM((tm // PEER_STRIP, D_MODEL, PEER_STRIP), F32),
            pltpu.VMEM((2, te, PEER_STRIP), F32),
            pltpu.VMEM((2, te, PEER_STRIP), BF16),
        ],
        compiler_params=_params("parallel", "arbitrary"),
        name="peer_dense",
    )(x, ht, e1, e2, thr, u, vt)


def _final_norm_kernel(x_ref, g_ref, o_ref):
    x = x_ref[...]
    o_ref[...] = x * lax.rsqrt(jnp.mean(x * x, axis=-1, keepdims=True) + EPS) * g_ref[...]


def _final_norm(x, g, *, tm=512):
    n, d = x.shape
    return pl.pallas_call(
        _final_norm_kernel,
        grid=(n // tm,),
        in_specs=[pl.BlockSpec((tm, d), lambda i: (i, 0)), pl.BlockSpec((1, d), lambda i: (0, 0))],
        out_specs=pl.BlockSpec((tm, d), lambda i: (i, 0)),
        out_shape=jax.ShapeDtypeStruct((n, d), F32),
        compiler_params=_params("parallel"),
        name="final_norm",
    )(x, g.reshape(1, d))


def _q_head_perm():
    cols = []
    for j in range(ATTN_HEADS // 2):
        for hd in (j, j + ATTN_HEADS // 2):
            cols.extend(range(hd * ATTN_HEAD_DIM, (hd + 1) * ATTN_HEAD_DIM))
    return np.asarray(cols, dtype=np.int32)


def _split_w_in(w):
    o = 0
    q = w[:, o:o + 512]; o += 512
    k = w[:, o:o + 128]; o += 128
    v = w[:, o:o + 128]; o += 128
    u = w[:, o:o + 512]; o += 512
    z = w[:, o:o + 1024]; o += 1024
    xs = w[:, o:o + 1024]; o += 1024
    bm = w[:, o:o + 128]; o += 128
    cm = w[:, o:o + 128]; o += 128
    dt = w[:, o:o + 16]; o += 16
    gates = w[:, o:o + 3072]
    q = q[:, _q_head_perm()]
    big = jnp.concatenate([z, xs, gates, q, u], axis=1).astype(BF16)
    small = jnp.concatenate([k, v, bm, cm, dt, jnp.zeros((w.shape[0], LANES - SSD_HEADS), w.dtype)], axis=1)
    return big, small.astype(BF16)


def kernel(x, mem, norm_mix, w_in, attn_sinks, s5_lambda_re, s5_lambda_im, s5_log_dt, s5_b_re, s5_b_im, s5_c_re, s5_c_im, s5_d, s5_glu_a, s5_glu_b, ssd_conv_w, ssd_conv_b, ssd_dt_bias, ssd_a_log, ssd_d, ssd_norm, w_attn_out, w_ssd_out, w_o, norm_xattn, norm_mem, xa_wq, xa_wkv, xa_wo, norm_ffn, peer_wq, peer_k1, peer_k2, peer_u, peer_v, norm_final):
    batch, seq, d = x.shape
    n = batch * seq
    n_mem = mem.shape[1]
    depth = w_in.shape[0]
    xf = x.reshape(n, d).astype(F32)
    memf = mem.reshape(batch * n_mem, d).astype(F32)
    for i in range(depth):
        w_big, w_small = _split_w_in(w_in[i])
        big = _norm_mm(xf, norm_mix[i], w_big, tm=1024, tn=1024, out_dtype=BF16, name="in_proj_big")
        small = _norm_mm(xf, norm_mix[i], w_small, tm=1024, tn=SMALL_COLS, out_dtype=F32, name="in_proj_small")
        attn = _swa(big, small, attn_sinks[i].astype(F32), batch=batch, seq=seq)
        wb, wc, sc = _s5_prepare(s5_lambda_re[i], s5_lambda_im[i], s5_log_dt[i], s5_b_re[i], s5_b_im[i],
                                 s5_c_re[i], s5_c_im[i])
        ys5 = _s5(big, wb, wc, s5_d[i].astype(F32), sc, batch=batch, seq=seq)
        ssd = _ssd(big, small, ssd_conv_w[i], ssd_conv_b[i], ssd_dt_bias[i], ssd_a_log[i], ssd_d[i], ssd_norm[i],
                   batch=batch, seq=seq)
        xf = _merge(attn, ys5, ssd, big, xf,
                    w_attn_out[i][_q_head_perm(), :].astype(BF16), s5_glu_a[i].astype(BF16),
                    s5_glu_b[i].astype(BF16), w_ssd_out[i].astype(BF16), w_o[i].astype(BF16))
        kv = _norm_mm(memf, norm_mem[i], xa_wkv[i].astype(BF16), tm=batch * n_mem, tn=2 * XA_WIDTH,
                      out_dtype=F32, name="mem_kv")
        xf = _xattn(xf, norm_xattn[i], kv, xa_wq[i].astype(BF16), xa_wo[i].astype(BF16),
                    batch=batch, seq=seq, n_mem=n_mem)
        q, ht = _norm_mm(xf, norm_ffn[i], peer_wq[i].astype(BF16), tm=1024, tn=1024, out_dtype=BF16,
                        with_h=True, name="peer_query")
        e1, e2, thr = _peer_route(q, peer_k1[i].astype(BF16), peer_k2[i].astype(BF16))
        xf = _peer_dense(xf, ht, e1, e2, thr, peer_u[i].astype(BF16), peer_v[i].astype(BF16).T)
    return _final_norm(xf, norm_final.astype(F32)).reshape(batch, seq, d)
```

```python
import functools
import math

import jax
import jax.numpy as jnp
import numpy as np
from jax import lax
from jax.experimental import pallas as pl
from jax.experimental.pallas import tpu as pltpu

F32 = jnp.float32
BF16 = jnp.bfloat16
NEG_INF = float("-inf")

D_MODEL = 1024
EPS = 1e-6
ATTN_HEADS = 8
ATTN_HEAD_DIM = 64
ATTN_WIDTH = ATTN_HEADS * ATTN_HEAD_DIM
ATTN_BLOCK = 128
S5_WIDTH = 512
S5_GROUP = 16
S5_GROUPS = S5_WIDTH // S5_GROUP
S5_STATE = 64
S5_LANES = S5_GROUPS * S5_STATE
S5_CHUNK = 256
S5_SCAN_WIDTH = 512
SSD_INNER = 1024
SSD_HEAD_DIM = 64
SSD_HEADS = SSD_INNER // SSD_HEAD_DIM
SSD_STATE = 64
SSD_GROUPS = 2
SSD_CONV = 4
SSD_CHUNK = 128
SSD_BC = SSD_GROUPS * SSD_STATE
SSD_CONV_DIM = SSD_INNER + 2 * SSD_BC
XA_HEADS = 4
XA_HEAD_DIM = 64
XA_WIDTH = XA_HEADS * XA_HEAD_DIM
PEER_HEADS = 8
PEER_KEY_DIM = 256
PEER_HALF = PEER_KEY_DIM // 2
PEER_N_KEYS = 128
PEER_TOPK = 16
PEER_N_EXPERTS = PEER_N_KEYS * PEER_N_KEYS
LANES = 128
SUBLANES = 8
VMEM_LIMIT_BYTES = 56 * 1024 * 1024

BIG_COLS = 6144
SMALL_COLS = 640


def _params(*sem):
    return pltpu.CompilerParams(dimension_semantics=sem, vmem_limit_bytes=VMEM_LIMIT_BYTES)


def _dot(a, b):
    return jnp.dot(a, b, preferred_element_type=F32)


def _dot_nt(a, b):
    return lax.dot_general(a, b, (((1,), (1,)), ((), ())), preferred_element_type=F32)


def _split3(x):
    p1 = x.astype(BF16)
    r1 = x - p1.astype(F32)
    p2 = r1.astype(BF16)
    p3 = (r1 - p2.astype(F32)).astype(BF16)
    return p1, p2, p3


def _exact_dot_rhs(sel, x):
    p1, p2, p3 = _split3(x)
    return _dot(sel, p1) + _dot(sel, p2) + _dot(sel, p3)


def _exact_dot_lhs(x, sel):
    p1, p2, p3 = _split3(x)
    return _dot(p1, sel) + _dot(p2, sel) + _dot(p3, sel)


def _norm_mm_kernel(x_ref, g_ref, w_ref, o_ref, *rest, with_h):
    if with_h:
        hout_ref, h_ref = rest
    else:
        (h_ref,) = rest

    @pl.when(pl.program_id(1) == 0)
    def _():
        x = x_ref[...]
        h = x * lax.rsqrt(jnp.mean(x * x, axis=-1, keepdims=True) + EPS) * g_ref[...]
        h_ref[...] = h.astype(BF16)
        if with_h:
            for s in range(hout_ref.shape[0]):
                hout_ref[s] = h[s * PEER_STRIP:(s + 1) * PEER_STRIP, :].T.astype(BF16)

    o_ref[...] = _dot(h_ref[...], w_ref[...]).astype(o_ref.dtype)


def _norm_mm(x, g, w, *, tm, tn, out_dtype, with_h=False, name):
    n, k = x.shape
    m = w.shape[1]
    assert n % tm == 0 and m % tn == 0
    out_shape = [jax.ShapeDtypeStruct((n, m), out_dtype)]
    out_specs = [pl.BlockSpec((tm, tn), lambda i, j: (i, j))]
    if with_h:
        out_shape.append(jax.ShapeDtypeStruct((n // PEER_STRIP, k, PEER_STRIP), BF16))
        out_specs.append(pl.BlockSpec((tm // PEER_STRIP, k, PEER_STRIP), lambda i, j: (i, 0, 0)))
    res = pl.pallas_call(
        functools.partial(_norm_mm_kernel, with_h=with_h),
        grid=(n // tm, m // tn),
        in_specs=[
            pl.BlockSpec((tm, k), lambda i, j: (i, 0)),
            pl.BlockSpec((1, k), lambda i, j: (0, 0)),
            pl.BlockSpec((k, tn), lambda i, j: (0, j)),
        ],
        out_specs=out_specs,
        out_shape=out_shape,
        scratch_shapes=[pltpu.VMEM((tm, k), BF16)],
        compiler_params=_params("parallel", "arbitrary"),
        name=name,
    )(x, g.reshape(1, k), w)
    return res if with_h else res[0]


def _swa_kernel(sink_ref, q_ref, kc_ref, kp_ref, vc_ref, vp_ref, o_ref):
    nblk = pl.program_id(1)
    blk = ATTN_BLOCK
    q = q_ref[...]
    k = jnp.concatenate([kp_ref[...], kc_ref[...]], axis=0)
    v = jnp.concatenate([vp_ref[...], vc_ref[...]], axis=0).astype(BF16)
    lane_k = lax.broadcasted_iota(jnp.int32, (2 * blk, LANES), 1)
    k_lo = jnp.where(lane_k < ATTN_HEAD_DIM, k, 0.0).astype(BF16)
    k_hi = jnp.where(lane_k < ATTN_HEAD_DIM, 0.0, k).astype(BF16)
    qi = lax.broadcasted_iota(jnp.int32, (blk, 2 * blk), 0)
    kj = lax.broadcasted_iota(jnp.int32, (blk, 2 * blk), 1)
    dist = qi - kj + blk
    valid = (dist >= 0) & (dist < blk) & ((nblk > 0) | (kj >= blk))
    distf = dist.astype(F32)
    lane_o = lax.broadcasted_iota(jnp.int32, (blk, LANES), 1)
    outs = []
    for j in range(ATTN_HEADS // 2):
        qs = (q[:, j * LANES:(j + 1) * LANES] * (ATTN_HEAD_DIM ** -0.5)).astype(BF16)
        res = []
        for half, kk in ((0, k_lo), (1, k_hi)):
            head = j + 4 * half
            slope = 2.0 ** (-8.0 * (head + 1) / ATTN_HEADS)
            s = _dot_nt(qs, kk) - slope * distf
            s = jnp.where(valid, s, NEG_INF)
            sink = sink_ref[head]
            m = jnp.maximum(jnp.max(s, axis=-1, keepdims=True), sink)
            p = jnp.exp(s - m)
            denom = jnp.sum(p, axis=-1, keepdims=True) + jnp.exp(sink - m)
            p = p / denom
            res.append(_dot(p.astype(BF16), v))
        outs.append(jnp.where(lane_o < ATTN_HEAD_DIM, res[0], res[1]))
    o_ref[...] = jnp.concatenate(outs, axis=1).astype(o_ref.dtype)


def _swa(big, small, sinks, *, batch, seq):
    nb = seq // ATTN_BLOCK
    n = batch * seq
    qcol = 5120 // ATTN_WIDTH

    def cur(col):
        return lambda b, i: (b * nb + i, col)

    def prev(col):
        return lambda b, i: (b * nb + jnp.maximum(i - 1, 0), col)

    return pl.pallas_call(
        _swa_kernel,
        grid=(batch, nb),
        in_specs=[
            pl.BlockSpec(memory_space=pltpu.SMEM),
            pl.BlockSpec((ATTN_BLOCK, ATTN_WIDTH), cur(qcol)),
            pl.BlockSpec((ATTN_BLOCK, LANES), cur(0)),
            pl.BlockSpec((ATTN_BLOCK, LANES), prev(0)),
            pl.BlockSpec((ATTN_BLOCK, LANES), cur(1)),
            pl.BlockSpec((ATTN_BLOCK, LANES), prev(1)),
        ],
        out_specs=pl.BlockSpec((ATTN_BLOCK, ATTN_WIDTH), lambda b, i: (b * nb + i, 0)),
        out_shape=jax.ShapeDtypeStruct((n, ATTN_WIDTH), BF16),
        compiler_params=_params("parallel", "arbitrary"),
        name="swa",
    )(sinks, big, small, small, small, small)


def _s5_kernel(u_ref, wb_ref, wc_ref, d_ref, sc_ref, o_ref, xr_ref, xi_ref, cr_ref, ci_ref):
    @pl.when(pl.program_id(1) == 0)
    def _():
        cr_ref[...] = jnp.zeros_like(cr_ref)
        ci_ref[...] = jnp.zeros_like(ci_ref)

    u = u_ref[...].astype(F32)
    bu = _dot(u_ref[...], wb_ref[...])
    xr_ref[...] = bu[:, :S5_LANES]
    xi_ref[...] = bu[:, S5_LANES:]

    def cmul_add(xr, xi, ar, ai, sr, si):
        return xr + (ar * sr - ai * si), xi + (ar * si + ai * sr)

    for w in range(S5_LANES // S5_SCAN_WIDTH):
        sl = slice(w * S5_SCAN_WIDTH, (w + 1) * S5_SCAN_WIDTH)
        consts = [sc_ref[c, :, sl] for c in range(8)]
        a1r, a1i, a2r, a2i, a4r, a4i, pr, pi = consts

        def body(t, carry):
            cr, ci = carry
            r0 = pl.multiple_of(t * SUBLANES, SUBLANES)
            xr = xr_ref[pl.ds(r0, SUBLANES), sl]
            xi = xi_ref[pl.ds(r0, SUBLANES), sl]
            xr, xi = cmul_add(xr, xi, a1r, a1i, pltpu.roll(xr, 1, 0), pltpu.roll(xi, 1, 0))
            xr, xi = cmul_add(xr, xi, a2r, a2i, pltpu.roll(xr, 2, 0), pltpu.roll(xi, 2, 0))
            xr, xi = cmul_add(xr, xi, a4r, a4i, pltpu.roll(xr, 4, 0), pltpu.roll(xi, 4, 0))
            xr, xi = cmul_add(xr, xi, pr, pi, cr, ci)
            xr_ref[pl.ds(r0, SUBLANES), sl] = xr
            xi_ref[pl.ds(r0, SUBLANES), sl] = xi
            return xr[SUBLANES - 1:SUBLANES, :], xi[SUBLANES - 1:SUBLANES, :]

        cr, ci = lax.fori_loop(0, S5_CHUNK // SUBLANES, body, (cr_ref[0:1, sl], ci_ref[0:1, sl]))
        cr_ref[0:1, sl] = cr
        ci_ref[0:1, sl] = ci

    xcat = jnp.concatenate([xr_ref[...].astype(BF16), xi_ref[...].astype(BF16)], axis=1)
    y = _dot(xcat, wc_ref[...]) + d_ref[...] * u
    o_ref[...] = jax.nn.gelu(y).astype(o_ref.dtype)


def _s5_prepare(lam_re, lam_im, log_dt, b_re, b_im, c_re, c_im):
    dt = jnp.exp(log_dt.astype(F32))[:, None]
    lr = lam_re.astype(F32)
    li = lam_im.astype(F32)
    mag = jnp.exp(lr * dt)
    ar = mag * jnp.cos(li * dt)
    ai = mag * jnp.sin(li * dt)
    den = lr * lr + li * li
    nr = ar - 1.0
    wr = (nr * lr + ai * li) / den
    wi = (ai * lr - nr * li) / den
    br_, bi_ = b_re.astype(F32), b_im.astype(F32)
    bbr = wr[..., None] * br_ - wi[..., None] * bi_
    bbi = wr[..., None] * bi_ + wi[..., None] * br_
    eye = jnp.eye(S5_GROUPS, dtype=F32)
    wbr = jnp.einsum('gpi,gh->gihp', bbr, eye).reshape(S5_WIDTH, S5_LANES)
    wbi = jnp.einsum('gpi,gh->gihp', bbi, eye).reshape(S5_WIDTH, S5_LANES)
    wb = jnp.concatenate([wbr, wbi], axis=1).astype(BF16)
    wcr = jnp.einsum('gip,gh->gphi', c_re.astype(F32), eye).reshape(S5_LANES, S5_WIDTH)
    wci = jnp.einsum('gip,gh->gphi', c_im.astype(F32), eye).reshape(S5_LANES, S5_WIDTH)
    wc = jnp.concatenate([wcr, -wci], axis=0).astype(BF16)
    ar_f = ar.reshape(1, S5_LANES)
    ai_f = ai.reshape(1, S5_LANES)

    def cmul(xr, xi, yr, yi):
        return xr * yr - xi * yi, xr * yi + xi * yr

    pows_r, pows_i = [ar_f], [ai_f]
    for _ in range(SUBLANES - 1):
        nr_, ni_ = cmul(pows_r[-1], pows_i[-1], ar_f, ai_f)
        pows_r.append(nr_)
        pows_i.append(ni_)
    t_idx = jnp.arange(SUBLANES)[:, None]

    def masked(k):
        return (jnp.where(t_idx >= k, pows_r[k - 1], 0.0), jnp.where(t_idx >= k, pows_i[k - 1], 0.0))

    a1 = masked(1)
    a2 = masked(2)
    a4 = masked(4)
    pr = jnp.concatenate(pows_r, axis=0)
    pi = jnp.concatenate(pows_i, axis=0)
    sc = jnp.stack([a1[0], a1[1], a2[0], a2[1], a4[0], a4[1], pr, pi], axis=0)
    return wb, wc, sc


def _s5(big, wb, wc, d, sc, *, batch, seq):
    n = batch * seq
    nc = seq // S5_CHUNK
    ucol = 5632 // S5_WIDTH
    return pl.pallas_call(
        _s5_kernel,
        grid=(batch, nc),
        in_specs=[
            pl.BlockSpec((S5_CHUNK, S5_WIDTH), lambda b, c: (b * nc + c, ucol)),
            pl.BlockSpec((S5_WIDTH, 2 * S5_LANES), lambda b, c: (0, 0)),
            pl.BlockSpec((2 * S5_LANES, S5_WIDTH), lambda b, c: (0, 0)),
            pl.BlockSpec((1, S5_WIDTH), lambda b, c: (0, 0)),
            pl.BlockSpec((8, SUBLANES, S5_LANES), lambda b, c: (0, 0, 0)),
        ],
        out_specs=pl.BlockSpec((S5_CHUNK, S5_WIDTH), lambda b, c: (b * nc + c, 0)),
        out_shape=jax.ShapeDtypeStruct((n, S5_WIDTH), BF16),
        scratch_shapes=[
            pltpu.VMEM((S5_CHUNK, S5_LANES), F32),
            pltpu.VMEM((S5_CHUNK, S5_LANES), F32),
            pltpu.VMEM((SUBLANES, S5_LANES), F32),
            pltpu.VMEM((SUBLANES, S5_LANES), F32),
        ],
        compiler_params=_params("parallel", "arbitrary"),
        name="s5",
    )(big, wb, wc, d.reshape(1, S5_WIDTH), sc)


def _ssd_kernel(z_ref, xs_ref, b_ref, c_ref, dt_ref, cw_ref, cb_ref, dtb_ref, alog_ref, dvec_ref, nw_ref,
                exp_h_ref, exp_l_ref, o_ref, buf_ref, st_ref):
    q = SSD_CHUNK

    @pl.when(pl.program_id(1) == 0)
    def _():
        buf_ref[0:SUBLANES, :] = jnp.zeros((SUBLANES, SSD_CONV_DIM), F32)
        st_ref[...] = jnp.zeros_like(st_ref)

    buf_ref[SUBLANES:SUBLANES + q, 0:SSD_INNER] = xs_ref[...].astype(F32)
    buf_ref[SUBLANES:SUBLANES + q, SSD_INNER:SSD_INNER + SSD_BC] = b_ref[...]
    buf_ref[SUBLANES:SUBLANES + q, SSD_INNER + SSD_BC:SSD_CONV_DIM] = c_ref[...]
    acc = jnp.broadcast_to(cb_ref[...], (q, SSD_CONV_DIM))
    for k in range(SSD_CONV):
        acc = acc + cw_ref[k:k + 1, :] * buf_ref[pl.ds(SUBLANES - (SSD_CONV - 1) + k, q), :]
    buf_ref[0:SUBLANES, :] = buf_ref[q:q + SUBLANES, :]
    xbc = acc * jax.nn.sigmoid(acc)
    xs = xbc[:, :SSD_INNER]
    bm = xbc[:, SSD_INNER:SSD_INNER + SSD_BC]
    cm = xbc[:, SSD_INNER + SSD_BC:]

    lane = lax.broadcasted_iota(jnp.int32, (q, LANES), 1)
    row = lax.broadcasted_iota(jnp.int32, (q, LANES), 0)
    head_lane = lane < SSD_HEADS
    dt_in = dt_ref[...] + dtb_ref[...]
    dt = jnp.maximum(dt_in, 0.0) + jnp.log1p(jnp.exp(-jnp.abs(dt_in)))
    dt = jnp.where(head_lane, dt, 0.0)
    a = -jnp.exp(alog_ref[...])
    ad = dt * a
    tri = (row >= lane).astype(BF16)
    cs = _exact_dot_rhs(tri, ad)
    cs_t = cs.T
    dt_t = dt.T
    total = cs[q - 1:q, :]
    dec_t = jnp.exp(cs_t[:, q - 1:q] - cs_t)
    w_t = dec_t * dt_t

    exp_h = exp_h_ref[...]
    exp_l = exp_l_ref[...]
    ecs_x = _exact_dot_lhs(jnp.exp(cs), exp_h)
    w_x = _exact_dot_lhs(w_t.T, exp_h)
    cs_cols = _exact_dot_lhs(cs, exp_l)
    tot_x = _exact_dot_lhs(jnp.broadcast_to(jnp.exp(total), (SUBLANES, LANES)), exp_h)[0:1, :]

    bm16 = bm.astype(BF16)
    cm16 = cm.astype(BF16)
    xs16 = xs.astype(BF16)
    low = lane < SSD_STATE
    g_mats = [_dot_nt(jnp.where(low, cm, 0.0).astype(BF16), bm16),
              _dot_nt(jnp.where(low, 0.0, cm).astype(BF16), bm16)]
    causal = row >= lane
    ydiag = []
    for j in range(SSD_HEADS // 2):
        slab = xs[:, j * LANES:(j + 1) * LANES]
        halves = (jnp.where(low, slab, 0.0).astype(BF16), jnp.where(low, 0.0, slab).astype(BF16))
        acc_j = None
        for half in range(2):
            h = 2 * j + half
            g = h // (SSD_HEADS // SSD_GROUPS)
            seg = cs_cols[:, h * LANES:(h + 1) * LANES] - cs_t[h:h + 1, :]
            lmat = jnp.exp(jnp.where(causal, seg, NEG_INF))
            m = (g_mats[g] * lmat * dt_t[h:h + 1, :]).astype(BF16)
            part = _dot(m, halves[half])
            acc_j = part if acc_j is None else acc_j + part
        ydiag.append(acc_j)
    y = jnp.concatenate(ydiag, axis=1)

    st = st_ref[...]
    y = y + _dot(cm16, st.astype(BF16)) * ecs_x
    new = _dot(bm.T.astype(BF16), (xs * w_x).astype(BF16))
    srow = lax.broadcasted_iota(jnp.int32, (LANES, SSD_INNER), 0)
    scol = lax.broadcasted_iota(jnp.int32, (LANES, SSD_INNER), 1)
    same_group = (srow < SSD_STATE) == (scol < SSD_INNER // SSD_GROUPS)
    st_ref[...] = jnp.where(same_group, st * tot_x + new, 0.0)

    y = y + dvec_ref[...] * xs
    zz = z_ref[...].astype(F32)
    gated = y * (zz * jax.nn.sigmoid(zz))
    out = gated * lax.rsqrt(jnp.mean(gated * gated, axis=-1, keepdims=True) + EPS) * nw_ref[...]
    o_ref[...] = out.astype(o_ref.dtype)


def _ssd(big, small, conv_w, conv_b, dt_bias, a_log, dvec, norm_w, *, batch, seq):
    n = batch * seq
    nc = seq // SSD_CHUNK
    q = SSD_CHUNK
    pad = LANES - SSD_HEADS
    dtb = jnp.pad(dt_bias.astype(F32), (0, pad)).reshape(1, LANES)
    alog = jnp.pad(a_log.astype(F32), (0, pad)).reshape(1, LANES)
    d_x = jnp.repeat(dvec.astype(F32), SSD_HEAD_DIM).reshape(1, SSD_INNER)
    heads = np.arange(LANES)[:, None]
    exp_h = jnp.asarray(heads == (np.arange(SSD_INNER)[None, :] // SSD_HEAD_DIM), dtype=BF16)
    exp_l = jnp.asarray(heads == (np.arange(SSD_HEADS * LANES)[None, :] // LANES), dtype=BF16)

    def blk(col, width):
        return pl.BlockSpec((q, width), lambda b, c: (b * nc + c, col))

    def full(shape):
        return pl.BlockSpec(shape, lambda b, c: (0,) * len(shape))

    return pl.pallas_call(
        _ssd_kernel,
        grid=(batch, nc),
        in_specs=[
            blk(0, SSD_INNER), blk(1, SSD_INNER),
            blk(2, LANES), blk(3, LANES), blk(4, LANES),
            full((SSD_CONV, SSD_CONV_DIM)), full((1, SSD_CONV_DIM)),
            full((1, LANES)), full((1, LANES)), full((1, SSD_INNER)), full((1, SSD_INNER)),
            full((LANES, SSD_INNER)), full((LANES, SSD_HEADS * LANES)),
        ],
        out_specs=pl.BlockSpec((q, SSD_INNER), lambda b, c: (b * nc + c, 0)),
        out_shape=jax.ShapeDtypeStruct((n, SSD_INNER), BF16),
        scratch_shapes=[
            pltpu.VMEM((q + SUBLANES, SSD_CONV_DIM), F32),
            pltpu.VMEM((LANES, SSD_INNER), F32),
        ],
        compiler_params=_params("parallel", "arbitrary"),
        name="ssd",
    )(big, big, small, small, small, conv_w.astype(F32), conv_b.astype(F32).reshape(1, SSD_CONV_DIM),
      dtb, alog, d_x, norm_w.astype(F32).reshape(1, SSD_INNER), exp_h, exp_l)


def _merge_kernel(attn_ref, ys5_ref, ssd_ref, ga_ref, gb_ref, gc_ref, x_ref,
                  wa_ref, wga_ref, wgb_ref, wc_ref, wo_ref, o_ref):
    ys5 = ys5_ref[...]
    br_a = _dot(attn_ref[...], wa_ref[...])
    br_b = _dot(ys5, wga_ref[...]) * jax.nn.sigmoid(_dot(ys5, wgb_ref[...]))
    br_c = _dot(ssd_ref[...], wc_ref[...])
    merged = (jax.nn.sigmoid(ga_ref[...].astype(F32)) * br_a + jax.nn.sigmoid(gb_ref[...].astype(F32)) * br_b
              + jax.nn.sigmoid(gc_ref[...].astype(F32)) * br_c)
    o_ref[...] = x_ref[...] + _dot(merged.astype(BF16), wo_ref[...])


def _merge(attn, ys5, ssd, big, x, wa, wga, wgb, wc, wo, *, tm=256):
    n = x.shape[0]

    def rows(width, col=0):
        return pl.BlockSpec((tm, width), lambda i: (i, col))

    def full(a):
        return pl.BlockSpec(a.shape, lambda i: (0, 0))

    return pl.pallas_call(
        _merge_kernel,
        grid=(n // tm,),
        in_specs=[rows(ATTN_WIDTH), rows(S5_WIDTH), rows(SSD_INNER),
                  rows(D_MODEL, 2), rows(D_MODEL, 3), rows(D_MODEL, 4), rows(D_MODEL),
                  full(wa), full(wga), full(wgb), full(wc), full(wo)],
        out_specs=rows(D_MODEL),
        out_shape=jax.ShapeDtypeStruct((n, D_MODEL), F32),
        compiler_params=_params("parallel"),
        name="merge",
    )(attn, ys5, ssd, big, big, big, x, wa, wga, wgb, wc, wo)


def _xattn_kernel(x_ref, g_ref, kv_ref, wq_ref, wo_ref, o_ref):
    x = x_ref[...]
    h = (x * lax.rsqrt(jnp.mean(x * x, axis=-1, keepdims=True) + EPS) * g_ref[...]).astype(BF16)
    q = (_dot(h, wq_ref[...]) * (XA_HEAD_DIM ** -0.5)).astype(BF16)
    kv = kv_ref[...]
    k = kv[:, :XA_WIDTH]
    v = kv[:, XA_WIDTH:]
    lane = lax.broadcasted_iota(jnp.int32, k.shape, 1)
    o = None
    for hd in range(XA_HEADS):
        sel = (lane >= hd * XA_HEAD_DIM) & (lane < (hd + 1) * XA_HEAD_DIM)
        kh = jnp.where(sel, k, 0.0).astype(BF16)
        vh = jnp.where(sel, v, 0.0).astype(BF16)
        s = _dot_nt(q, kh)
        s = s - jnp.max(s, axis=-1, keepdims=True)
        p = jnp.exp(s)
        p = p / jnp.sum(p, axis=-1, keepdims=True)
        part = _dot(p.astype(BF16), vh)
        o = part if o is None else o + part
    o_ref[...] = x + _dot(o.astype(BF16), wo_ref[...])


def _xattn(x, g, kv, wq, wo, *, batch, seq, n_mem, tm=256):
    n = batch * seq
    nt = seq // tm
    return pl.pallas_call(
        _xattn_kernel,
        grid=(batch, nt),
        in_specs=[
            pl.BlockSpec((tm, D_MODEL), lambda b, i: (b * nt + i, 0)),
            pl.BlockSpec((1, D_MODEL), lambda b, i: (0, 0)),
            pl.BlockSpec((n_mem, 2 * XA_WIDTH), lambda b, i: (b, 0)),
            pl.BlockSpec((D_MODEL, XA_WIDTH), lambda b, i: (0, 0)),
            pl.BlockSpec((XA_WIDTH, D_MODEL), lambda b, i: (0, 0)),
        ],
        out_specs=pl.BlockSpec((tm, D_MODEL), lambda b, i: (b * nt + i, 0)),
        out_shape=jax.ShapeDtypeStruct((n, D_MODEL), F32),
        compiler_params=_params("parallel", "parallel"),
        name="xattn",
    )(x, g.reshape(1, D_MODEL), kv, wq, wo)


_PEER_CELLS = [(i, j) for i in range(PEER_TOPK) for j in range(PEER_TOPK) if (i + 1) * (j + 1) <= PEER_TOPK]


def _peer_route_kernel(q_ref, k1_ref, k2_ref, e1_ref, e2_ref, thr_ref, rows_ref, kept_ref):
    tm = q_ref.shape[0]
    keys = (k1_ref[...], k2_ref[...])
    kidx = lax.broadcasted_iota(jnp.int32, (PEER_N_KEYS, tm), 0).astype(F32)

    def store(which, hd, st, rows, vals):
        for r in range(PEER_TOPK):
            rows_ref[which, r, hd:hd + 1, :] = rows[r] - rows[0]
        kept_ref[which, hd] = jnp.where(vals == NEG_INF, st - rows[0], NEG_INF)

    def scores(hd, which):
        c0 = hd * PEER_KEY_DIM + which * PEER_HALF
        return _dot_nt(keys[which], q_ref[:, c0:c0 + PEER_HALF])

    tied = jnp.zeros((1, tm), F32)
    for hd in range(PEER_HEADS):
        for which in range(2):
            st = scores(hd, which)
            vals = st
            rows = []
            for _ in range(PEER_TOPK):
                m = jnp.max(vals, axis=0, keepdims=True)
                vals = jnp.where(vals == m, NEG_INF, vals)
                rows.append(m)
            store(which, hd, st, rows, vals)
            removed = jnp.sum(jnp.where(vals == NEG_INF, 1.0, 0.0), axis=0, keepdims=True)
            tied = jnp.maximum(tied, jnp.abs(removed - float(PEER_TOPK)))

    @pl.when(jnp.max(tied) > 0.0)
    def _():
        for hd in range(PEER_HEADS):
            for which in range(2):
                st = scores(hd, which)
                vals = st
                rows = []
                for _ in range(PEER_TOPK):
                    m = jnp.max(vals, axis=0, keepdims=True)
                    first = jnp.min(jnp.where(vals == m, kidx, float(PEER_N_KEYS)), axis=0, keepdims=True)
                    vals = jnp.where(kidx == first, NEG_INF, vals)
                    rows.append(m)
                store(which, hd, st, rows, vals)

    v1 = [rows_ref[0, r] for r in range(PEER_TOPK)]
    v2 = [rows_ref[1, r] for r in range(PEER_TOPK)]
    cands = [v1[i] + v2[j] for (i, j) in _PEER_CELLS]
    work = cands
    cum = jnp.zeros_like(v1[0])
    tau = jnp.full_like(v1[0], NEG_INF)
    zsum = jnp.zeros_like(v1[0])
    for _ in range(PEER_TOPK):
        m = functools.reduce(jnp.maximum, work)
        cnt = jnp.zeros_like(m)
        nxt = []
        for c in work:
            eq = c == m
            cnt = cnt + jnp.where(eq, 1.0, 0.0)
            nxt.append(jnp.where(eq, NEG_INF, c))
        work = nxt
        open_ = cum < PEER_TOPK
        used = jnp.minimum(cnt, PEER_TOPK - cum)
        zsum = zsum + jnp.where(open_, used * jnp.exp(m), 0.0)
        tau = jnp.where(open_, m, tau)
        cum = cum + cnt
    log_z = jnp.log(zsum)
    e1v = [jnp.exp(v - log_z) for v in v1]
    e2v = [jnp.exp(v) for v in v2]
    thr = jnp.full_like(tau, jnp.inf)
    for (i, j), c in zip(_PEER_CELLS, cands):
        thr = jnp.minimum(thr, jnp.where(c >= tau, e1v[i] * e2v[j], jnp.inf))
    thr_ref[0] = thr
    for hd in range(PEER_HEADS):
        e1_ref[hd, 0] = jnp.exp(kept_ref[0, hd] - log_z[hd:hd + 1, :])
        e2_ref[hd, 0] = jnp.exp(kept_ref[1, hd])


def _peer_route(q, k1, k2):
    n = q.shape[0]
    tm = LANES
    ns = n // tm
    big_shape = jax.ShapeDtypeStruct((PEER_HEADS, ns, PEER_N_KEYS, tm), F32)
    big_spec = pl.BlockSpec((PEER_HEADS, 1, PEER_N_KEYS, tm), lambda i: (0, i, 0, 0))
    return pl.pallas_call(
        _peer_route_kernel,
        grid=(ns,),
        in_specs=[
            pl.BlockSpec((tm, PEER_HEADS * PEER_KEY_DIM), lambda i: (i, 0)),
            pl.BlockSpec((PEER_N_KEYS, PEER_HALF), lambda i: (0, 0)),
            pl.BlockSpec((PEER_N_KEYS, PEER_HALF), lambda i: (0, 0)),
        ],
        out_specs=[big_spec, big_spec, pl.BlockSpec((1, PEER_HEADS, tm), lambda i: (i, 0, 0))],
        out_shape=[big_shape, big_shape, jax.ShapeDtypeStruct((ns, PEER_HEADS, tm), F32)],
        scratch_shapes=[
            pltpu.VMEM((2, PEER_TOPK, PEER_HEADS, tm), F32),
            pltpu.VMEM((2, PEER_HEADS, PEER_N_KEYS, tm), F32),
        ],
        compiler_params=_params("parallel"),
        name="peer_route",
    )(q, k1, k2)


PEER_TE = 1024
PEER_TM = 1024
PEER_STRIP = 256
PEER_KEY_CHUNK = 32


def _gelu(x):
    c = math.sqrt(2.0 / math.pi)
    inner = x * ((x * x) * (0.044715 * c) + c)
    hx = 0.5 * x
    return hx + hx * jnp.tanh(inner)


def _peer_dense_kernel(x_ref, ht_ref, e1_ref, e2_ref, thr_ref, u_ref, vt_ref, o_ref, acc_ref, act_ref, p_ref):
    j = pl.program_id(1)
    n_slab = PEER_TE // PEER_N_KEYS

    @pl.when(j == 0)
    def _():
        acc_ref[...] = jnp.zeros_like(acc_ref)

    u = u_ref[...]
    vt = vt_ref[...]
    wide = PEER_STRIP
    n_strip = PEER_TM // wide

    halves = wide // LANES
    n_chunk = PEER_N_KEYS // PEER_KEY_CHUNK

    for s in range(n_strip):
        res = _dot(u, ht_ref[s])
        for half in range(halves):
            act_ref[s * halves + half] = res[:, half * LANES:(half + 1) * LANES]

    def build(idx, carry):
        tc = idx // n_chunk
        k0 = pl.multiple_of((idx % n_chunk) * PEER_KEY_CHUNK, PEER_KEY_CHUNK)
        thr = thr_ref[tc]
        ws = [None] * n_slab
        for hd in range(PEER_HEADS):
            e2c = e2_ref[hd, tc, pl.ds(k0, PEER_KEY_CHUNK), :]
            thr_h = thr[hd:hd + 1, :]
            for al in range(n_slab):
                a = j * n_slab + al
                prod = e1_ref[hd, tc, pl.ds(a, 1), :] * e2c
                contrib = jnp.where(prod >= thr_h, prod, 0.0)
                ws[al] = contrib if ws[al] is None else ws[al] + contrib
        for al in range(n_slab):
            r0 = pl.multiple_of(al * PEER_N_KEYS + k0, PEER_KEY_CHUNK)
            g = _gelu(act_ref[tc, pl.ds(r0, PEER_KEY_CHUNK), :])
            p_ref[tc, pl.ds(r0, PEER_KEY_CHUNK), :] = (ws[al] * g).astype(BF16)
        return carry

    lax.fori_loop(0, n_strip * halves * n_chunk, build, 0)

    for s in range(n_strip):
        p = jnp.concatenate([p_ref[s * halves + half] for half in range(halves)], axis=1)
        acc_ref[s] += _dot(vt, p)

    @pl.when(j == pl.num_programs(1) - 1)
    def _():
        for s in range(n_strip):
            o_ref[s * wide:(s + 1) * wide, :] = x_ref[s * wide:(s + 1) * wide, :] + acc_ref[s].T


def _peer_dense(x, ht, e1, e2, thr, u, vt):
    n = x.shape[0]
    tm, te = PEER_TM, PEER_TE
    ns = tm // LANES
    big_spec = pl.BlockSpec((PEER_HEADS, ns, PEER_N_KEYS, LANES), lambda i, j: (0, i, 0, 0),
                            pipeline_mode=pl.Buffered(1))
    return pl.pallas_call(
        _peer_dense_kernel,
        grid=(n // tm, PEER_N_EXPERTS // te),
        in_specs=[
            pl.BlockSpec((tm, D_MODEL), lambda i, j: (i, 0)),
            pl.BlockSpec((tm // PEER_STRIP, D_MODEL, PEER_STRIP), lambda i, j: (i, 0, 0)),
            big_spec, big_spec,
            pl.BlockSpec((ns, PEER_HEADS, LANES), lambda i, j: (i, 0, 0)),
            pl.BlockSpec((te, D_MODEL), lambda i, j: (j, 0)),
            pl.BlockSpec((D_MODEL, te), lambda i, j: (0, j)),
        ],
        out_specs=pl.BlockSpec((tm, D_MODEL), lambda i, j: (i, 0)),
        out_shape=jax.ShapeDtypeStruct((n, D_MODEL), F32),
        scratch_shapes=[
            pltpu.VMEM((tm // PEER_STRIP, D_MODEL, PEER_STRIP), F32),
            pltpu.VMEM((ns, te, LANES), F32),
            pltpu.VMEM((ns, te, LANES), BF16),
        ],
        compiler_params=_params("parallel", "arbitrary"),
        name="peer_dense",
    )(x, ht, e1, e2, thr, u, vt)


def _final_norm_kernel(x_ref, g_ref, o_ref):
    x = x_ref[...]
    o_ref[...] = x * lax.rsqrt(jnp.mean(x * x, axis=-1, keepdims=True) + EPS) * g_ref[...]


def _final_norm(x, g, *, tm=512):
    n, d = x.shape
    return pl.pallas_call(
        _final_norm_kernel,
        grid=(n // tm,),
        in_specs=[pl.BlockSpec((tm, d), lambda i: (i, 0)), pl.BlockSpec((1, d), lambda i: (0, 0))],
        out_specs=pl.BlockSpec((tm, d), lambda i: (i, 0)),
        out_shape=jax.ShapeDtypeStruct((n, d), F32),
        compiler_params=_params("parallel"),
        name="final_norm",
    )(x, g.reshape(1, d))


def _q_head_perm():
    cols = []
    for j in range(ATTN_HEADS // 2):
        for hd in (j, j + ATTN_HEADS // 2):
            cols.extend(range(hd * ATTN_HEAD_DIM, (hd + 1) * ATTN_HEAD_DIM))
    return np.asarray(cols, dtype=np.int32)


def _split_w_in(w):
    o = 0
    q = w[:, o:o + 512]; o += 512
    k = w[:, o:o + 128]; o += 128
    v = w[:, o:o + 128]; o += 128
    u = w[:, o:o + 512]; o += 512
    z = w[:, o:o + 1024]; o += 1024
    xs = w[:, o:o + 1024]; o += 1024
    bm = w[:, o:o + 128]; o += 128
    cm = w[:, o:o + 128]; o += 128
    dt = w[:, o:o + 16]; o += 16
    gates = w[:, o:o + 3072]
    q = q[:, _q_head_perm()]
    big = jnp.concatenate([z, xs, gates, q, u], axis=1).astype(BF16)
    small = jnp.concatenate([k, v, bm, cm, dt, jnp.zeros((w.shape[0], LANES - SSD_HEADS), w.dtype)], axis=1)
    return big, small.astype(BF16)


def kernel(x, mem, norm_mix, w_in, attn_sinks, s5_lambda_re, s5_lambda_im, s5_log_dt, s5_b_re, s5_b_im, s5_c_re, s5_c_im, s5_d, s5_glu_a, s5_glu_b, ssd_conv_w, ssd_conv_b, ssd_dt_bias, ssd_a_log, ssd_d, ssd_norm, w_attn_out, w_ssd_out, w_o, norm_xattn, norm_mem, xa_wq, xa_wkv, xa_wo, norm_ffn, peer_wq, peer_k1, peer_k2, peer_u, peer_v, norm_final):
    batch, seq, d = x.shape
    n = batch * seq
    n_mem = mem.shape[1]
    depth = w_in.shape[0]
    xf = x.reshape(n, d).astype(F32)
    memf = mem.reshape(batch * n_mem, d).astype(F32)
    for i in range(depth):
        w_big, w_small = _split_w_in(w_in[i])
        big = _norm_mm(xf, norm_mix[i], w_big, tm=1024, tn=1024, out_dtype=BF16, name="in_proj_big")
        small = _norm_mm(xf, norm_mix[i], w_small, tm=1024, tn=SMALL_COLS, out_dtype=F32, name="in_proj_small")
        attn = _swa(big, small, attn_sinks[i].astype(F32), batch=batch, seq=seq)
        wb, wc, sc = _s5_prepare(s5_lambda_re[i], s5_lambda_im[i], s5_log_dt[i], s5_b_re[i], s5_b_im[i],
                                 s5_c_re[i], s5_c_im[i])
        ys5 = _s5(big, wb, wc, s5_d[i].astype(F32), sc, batch=batch, seq=seq)
        ssd = _ssd(big, small, ssd_conv_w[i], ssd_conv_b[i], ssd_dt_bias[i], ssd_a_log[i], ssd_d[i], ssd_norm[i],
                   batch=batch, seq=seq)
        xf = _merge(attn, ys5, ssd, big, xf,
                    w_attn_out[i][_q_head_perm(), :].astype(BF16), s5_glu_a[i].astype(BF16),
                    s5_glu_b[i].astype(BF16), w_ssd_out[i].astype(BF16), w_o[i].astype(BF16))
        kv = _norm_mm(memf, norm_mem[i], xa_wkv[i].astype(BF16), tm=batch * n_mem, tn=2 * XA_WIDTH,
                      out_dtype=F32, name="mem_kv")
        xf = _xattn(xf, norm_xattn[i], kv, xa_wq[i].astype(BF16), xa_wo[i].astype(BF16),
                    batch=batch, seq=seq, n_mem=n_mem)
        q, ht = _norm_mm(xf, norm_ffn[i], peer_wq[i].astype(BF16), tm=1024, tn=1024, out_dtype=BF16,
                        with_h=True, name="peer_query")
        e1, e2, thr = _peer_route(q, peer_k1[i].astype(BF16), peer_k2[i].astype(BF16))
        xf = _peer_dense(xf, ht, e1, e2, thr, peer_u[i].astype(BF16), peer_v[i].astype(BF16).T)
    return _final_norm(xf, norm_final.astype(F32)).reshape(batch, seq, d)
```

```python
import functools
import math

import jax
import jax.numpy as jnp
import numpy as np
from jax import lax
from jax.experimental import pallas as pl
from jax.experimental.pallas import tpu as pltpu

F32 = jnp.float32
BF16 = jnp.bfloat16
NEG_INF = float("-inf")

D_MODEL = 1024
EPS = 1e-6
ATTN_HEADS = 8
ATTN_HEAD_DIM = 64
ATTN_WIDTH = ATTN_HEADS * ATTN_HEAD_DIM
ATTN_BLOCK = 128
S5_WIDTH = 512
S5_GROUP = 16
S5_GROUPS = S5_WIDTH // S5_GROUP
S5_STATE = 64
S5_LANES = S5_GROUPS * S5_STATE
S5_CHUNK = 256
S5_SCAN_WIDTH = 512
SSD_INNER = 1024
SSD_HEAD_DIM = 64
SSD_HEADS = SSD_INNER // SSD_HEAD_DIM
SSD_STATE = 64
SSD_GROUPS = 2
SSD_CONV = 4
SSD_CHUNK = 128
SSD_BC = SSD_GROUPS * SSD_STATE
SSD_CONV_DIM = SSD_INNER + 2 * SSD_BC
XA_HEADS = 4
XA_HEAD_DIM = 64
XA_WIDTH = XA_HEADS * XA_HEAD_DIM
PEER_HEADS = 8
PEER_KEY_DIM = 256
PEER_HALF = PEER_KEY_DIM // 2
PEER_N_KEYS = 128
PEER_TOPK = 16
PEER_N_EXPERTS = PEER_N_KEYS * PEER_N_KEYS
LANES = 128
SUBLANES = 8
VMEM_LIMIT_BYTES = 56 * 1024 * 1024
S5_BLOCKS = S5_WIDTH // LANES

BIG_COLS = 6144
SMALL_COLS = 640


def _params(*sem):
    return pltpu.CompilerParams(dimension_semantics=sem, vmem_limit_bytes=VMEM_LIMIT_BYTES)


def _dot(a, b):
    return jnp.dot(a, b, preferred_element_type=F32)


def _dot_nt(a, b):
    return lax.dot_general(a, b, (((1,), (1,)), ((), ())), preferred_element_type=F32)


def _split3(x):
    p1 = x.astype(BF16)
    r1 = x - p1.astype(F32)
    p2 = r1.astype(BF16)
    p3 = (r1 - p2.astype(F32)).astype(BF16)
    return p1, p2, p3


def _exact_dot_rhs(sel, x):
    p1, p2, p3 = _split3(x)
    return _dot(sel, p1) + _dot(sel, p2) + _dot(sel, p3)


def _exact_dot_lhs(x, sel):
    p1, p2, p3 = _split3(x)
    return _dot(p1, sel) + _dot(p2, sel) + _dot(p3, sel)


def _norm_mm_kernel(x_ref, g_ref, w_ref, o_ref, *rest, with_h):
    if with_h:
        hout_ref, h_ref = rest
    else:
        (h_ref,) = rest

    @pl.when(pl.program_id(1) == 0)
    def _():
        x = x_ref[...]
        h = x * lax.rsqrt(jnp.mean(x * x, axis=-1, keepdims=True) + EPS) * g_ref[...]
        h_ref[...] = h.astype(BF16)
        if with_h:
            for s in range(hout_ref.shape[0]):
                hout_ref[s] = h[s * PEER_STRIP:(s + 1) * PEER_STRIP, :].T.astype(BF16)

    o_ref[...] = _dot(h_ref[...], w_ref[...]).astype(o_ref.dtype)


def _norm_mm(x, g, w, *, tm, tn, out_dtype, with_h=False, name):
    n, k = x.shape
    m = w.shape[1]
    assert n % tm == 0 and m % tn == 0
    out_shape = [jax.ShapeDtypeStruct((n, m), out_dtype)]
    out_specs = [pl.BlockSpec((tm, tn), lambda i, j: (i, j))]
    if with_h:
        out_shape.append(jax.ShapeDtypeStruct((n // PEER_STRIP, k, PEER_STRIP), BF16))
        out_specs.append(pl.BlockSpec((tm // PEER_STRIP, k, PEER_STRIP), lambda i, j: (i, 0, 0)))
    res = pl.pallas_call(
        functools.partial(_norm_mm_kernel, with_h=with_h),
        grid=(n // tm, m // tn),
        in_specs=[
            pl.BlockSpec((tm, k), lambda i, j: (i, 0)),
            pl.BlockSpec((1, k), lambda i, j: (0, 0)),
            pl.BlockSpec((k, tn), lambda i, j: (0, j)),
        ],
        out_specs=out_specs,
        out_shape=out_shape,
        scratch_shapes=[pltpu.VMEM((tm, k), BF16)],
        compiler_params=_params("parallel", "arbitrary"),
        name=name,
    )(x, g.reshape(1, k), w)
    return res if with_h else res[0]


def _swa_kernel(sink_ref, q_ref, kc_ref, kp_ref, vc_ref, vp_ref, o_ref):
    nblk = pl.program_id(1)
    blk = ATTN_BLOCK
    q = q_ref[...]
    k = jnp.concatenate([kp_ref[...], kc_ref[...]], axis=0)
    v = jnp.concatenate([vp_ref[...], vc_ref[...]], axis=0).astype(BF16)
    lane_k = lax.broadcasted_iota(jnp.int32, (2 * blk, LANES), 1)
    k_lo = jnp.where(lane_k < ATTN_HEAD_DIM, k, 0.0).astype(BF16)
    k_hi = jnp.where(lane_k < ATTN_HEAD_DIM, 0.0, k).astype(BF16)
    qi = lax.broadcasted_iota(jnp.int32, (blk, 2 * blk), 0)
    kj = lax.broadcasted_iota(jnp.int32, (blk, 2 * blk), 1)
    dist = qi - kj + blk
    valid = (dist >= 0) & (dist < blk) & ((nblk > 0) | (kj >= blk))
    distf = dist.astype(F32)
    lane_o = lax.broadcasted_iota(jnp.int32, (blk, LANES), 1)
    outs = []
    for j in range(ATTN_HEADS // 2):
        qs = (q[:, j * LANES:(j + 1) * LANES] * (ATTN_HEAD_DIM ** -0.5)).astype(BF16)
        res = []
        for half, kk in ((0, k_lo), (1, k_hi)):
            head = j + 4 * half
            slope = 2.0 ** (-8.0 * (head + 1) / ATTN_HEADS)
            s = _dot_nt(qs, kk) - slope * distf
            s = jnp.where(valid, s, NEG_INF)
            sink = sink_ref[head]
            m = jnp.maximum(jnp.max(s, axis=-1, keepdims=True), sink)
            p = jnp.exp(s - m)
            denom = jnp.sum(p, axis=-1, keepdims=True) + jnp.exp(sink - m)
            p = p / denom
            res.append(_dot(p.astype(BF16), v))
        outs.append(jnp.where(lane_o < ATTN_HEAD_DIM, res[0], res[1]))
    o_ref[...] = jnp.concatenate(outs, axis=1).astype(o_ref.dtype)


def _swa(big, small, sinks, *, batch, seq):
    nb = seq // ATTN_BLOCK
    n = batch * seq
    qcol = 5120 // ATTN_WIDTH

    def cur(col):
        return lambda b, i: (b * nb + i, col)

    def prev(col):
        return lambda b, i: (b * nb + jnp.maximum(i - 1, 0), col)

    return pl.pallas_call(
        _swa_kernel,
        grid=(batch, nb),
        in_specs=[
            pl.BlockSpec(memory_space=pltpu.SMEM),
            pl.BlockSpec((ATTN_BLOCK, ATTN_WIDTH), cur(qcol)),
            pl.BlockSpec((ATTN_BLOCK, LANES), cur(0)),
            pl.BlockSpec((ATTN_BLOCK, LANES), prev(0)),
            pl.BlockSpec((ATTN_BLOCK, LANES), cur(1)),
            pl.BlockSpec((ATTN_BLOCK, LANES), prev(1)),
        ],
        out_specs=pl.BlockSpec((ATTN_BLOCK, ATTN_WIDTH), lambda b, i: (b * nb + i, 0)),
        out_shape=jax.ShapeDtypeStruct((n, ATTN_WIDTH), BF16),
        compiler_params=_params("parallel", "arbitrary"),
        name="swa",
    )(sinks, big, small, small, small, small)


def _s5_kernel(u_ref, wb_ref, wc_ref, d_ref, sc_ref, o_ref, xr_ref, xi_ref, cr_ref, ci_ref):
    @pl.when(pl.program_id(1) == 0)
    def _():
        cr_ref[...] = jnp.zeros_like(cr_ref)
        ci_ref[...] = jnp.zeros_like(ci_ref)

    u = u_ref[...].astype(F32)
    ub = u_ref[...]
    blk_states = S5_LANES // S5_BLOCKS
    for jb in range(S5_BLOCKS):
        st = slice(jb * blk_states, (jb + 1) * blk_states)
        bu = _dot(ub[:, jb * LANES:(jb + 1) * LANES], wb_ref[jb])
        xr_ref[:, st] = bu[:, :blk_states]
        xi_ref[:, st] = bu[:, blk_states:]

    def cmul_add(xr, xi, ar, ai, sr, si):
        return xr + (ar * sr - ai * si), xi + (ar * si + ai * sr)

    for w in range(S5_LANES // S5_SCAN_WIDTH):
        sl = slice(w * S5_SCAN_WIDTH, (w + 1) * S5_SCAN_WIDTH)
        consts = [sc_ref[c, :, sl] for c in range(8)]
        a1r, a1i, a2r, a2i, a4r, a4i, pr, pi = consts

        def body(t, carry):
            cr, ci = carry
            r0 = pl.multiple_of(t * SUBLANES, SUBLANES)
            xr = xr_ref[pl.ds(r0, SUBLANES), sl]
            xi = xi_ref[pl.ds(r0, SUBLANES), sl]
            xr, xi = cmul_add(xr, xi, a1r, a1i, pltpu.roll(xr, 1, 0), pltpu.roll(xi, 1, 0))
            xr, xi = cmul_add(xr, xi, a2r, a2i, pltpu.roll(xr, 2, 0), pltpu.roll(xi, 2, 0))
            xr, xi = cmul_add(xr, xi, a4r, a4i, pltpu.roll(xr, 4, 0), pltpu.roll(xi, 4, 0))
            xr, xi = cmul_add(xr, xi, pr, pi, cr, ci)
            xr_ref[pl.ds(r0, SUBLANES), sl] = xr
            xi_ref[pl.ds(r0, SUBLANES), sl] = xi
            return xr[SUBLANES - 1:SUBLANES, :], xi[SUBLANES - 1:SUBLANES, :]

        cr, ci = lax.fori_loop(0, S5_CHUNK // SUBLANES, body, (cr_ref[0:1, sl], ci_ref[0:1, sl]), unroll=4)
        cr_ref[0:1, sl] = cr
        ci_ref[0:1, sl] = ci

    ys = []
    for jb in range(S5_BLOCKS):
        st = slice(jb * blk_states, (jb + 1) * blk_states)
        xcat = jnp.concatenate([xr_ref[:, st].astype(BF16), xi_ref[:, st].astype(BF16)], axis=1)
        ys.append(_dot(xcat, wc_ref[jb]))
    y = jnp.concatenate(ys, axis=1) + d_ref[...] * u
    o_ref[...] = jax.nn.gelu(y).astype(o_ref.dtype)


def _s5_prepare(lam_re, lam_im, log_dt, b_re, b_im, c_re, c_im):
    dt = jnp.exp(log_dt.astype(F32))[:, None]
    lr = lam_re.astype(F32)
    li = lam_im.astype(F32)
    mag = jnp.exp(lr * dt)
    ar = mag * jnp.cos(li * dt)
    ai = mag * jnp.sin(li * dt)
    den = lr * lr + li * li
    nr = ar - 1.0
    wr = (nr * lr + ai * li) / den
    wi = (ai * lr - nr * li) / den
    br_, bi_ = b_re.astype(F32), b_im.astype(F32)
    bbr = wr[..., None] * br_ - wi[..., None] * bi_
    bbi = wr[..., None] * bi_ + wi[..., None] * br_
    eye = jnp.eye(S5_GROUPS, dtype=F32)
    wbr = jnp.einsum('gpi,gh->gihp', bbr, eye).reshape(S5_WIDTH, S5_LANES)
    wbi = jnp.einsum('gpi,gh->gihp', bbi, eye).reshape(S5_WIDTH, S5_LANES)
    bs = S5_LANES // S5_BLOCKS
    wb = jnp.stack([jnp.concatenate([wbr[jb * LANES:(jb + 1) * LANES, jb * bs:(jb + 1) * bs],
                                     wbi[jb * LANES:(jb + 1) * LANES, jb * bs:(jb + 1) * bs]], axis=1)
                    for jb in range(S5_BLOCKS)]).astype(BF16)
    wcr = jnp.einsum('gip,gh->gphi', c_re.astype(F32), eye).reshape(S5_LANES, S5_WIDTH)
    wci = jnp.einsum('gip,gh->gphi', c_im.astype(F32), eye).reshape(S5_LANES, S5_WIDTH)
    wc = jnp.stack([jnp.concatenate([wcr[jb * bs:(jb + 1) * bs, jb * LANES:(jb + 1) * LANES],
                                     -wci[jb * bs:(jb + 1) * bs, jb * LANES:(jb + 1) * LANES]], axis=0)
                    for jb in range(S5_BLOCKS)]).astype(BF16)
    ar_f = ar.reshape(1, S5_LANES)
    ai_f = ai.reshape(1, S5_LANES)

    def cmul(xr, xi, yr, yi):
        return xr * yr - xi * yi, xr * yi + xi * yr

    pows_r, pows_i = [ar_f], [ai_f]
    for _ in range(SUBLANES - 1):
        nr_, ni_ = cmul(pows_r[-1], pows_i[-1], ar_f, ai_f)
        pows_r.append(nr_)
        pows_i.append(ni_)
    t_idx = jnp.arange(SUBLANES)[:, None]

    def masked(k):
        return (jnp.where(t_idx >= k, pows_r[k - 1], 0.0), jnp.where(t_idx >= k, pows_i[k - 1], 0.0))

    a1 = masked(1)
    a2 = masked(2)
    a4 = masked(4)
    pr = jnp.concatenate(pows_r, axis=0)
    pi = jnp.concatenate(pows_i, axis=0)
    sc = jnp.stack([a1[0], a1[1], a2[0], a2[1], a4[0], a4[1], pr, pi], axis=0)
    return wb, wc, sc


def _s5(big, wb, wc, d, sc, *, batch, seq):
    n = batch * seq
    nc = seq // S5_CHUNK
    ucol = 5632 // S5_WIDTH
    return pl.pallas_call(
        _s5_kernel,
        grid=(batch, nc),
        in_specs=[
            pl.BlockSpec((S5_CHUNK, S5_WIDTH), lambda b, c: (b * nc + c, ucol)),
            pl.BlockSpec((S5_BLOCKS, LANES, 2 * S5_LANES // S5_BLOCKS), lambda b, c: (0, 0, 0)),
            pl.BlockSpec((S5_BLOCKS, 2 * S5_LANES // S5_BLOCKS, LANES), lambda b, c: (0, 0, 0)),
            pl.BlockSpec((1, S5_WIDTH), lambda b, c: (0, 0)),
            pl.BlockSpec((8, SUBLANES, S5_LANES), lambda b, c: (0, 0, 0)),
        ],
        out_specs=pl.BlockSpec((S5_CHUNK, S5_WIDTH), lambda b, c: (b * nc + c, 0)),
        out_shape=jax.ShapeDtypeStruct((n, S5_WIDTH), BF16),
        scratch_shapes=[
            pltpu.VMEM((S5_CHUNK, S5_LANES), F32),
            pltpu.VMEM((S5_CHUNK, S5_LANES), F32),
            pltpu.VMEM((SUBLANES, S5_LANES), F32),
            pltpu.VMEM((SUBLANES, S5_LANES), F32),
        ],
        compiler_params=_params("parallel", "arbitrary"),
        name="s5",
    )(big, wb, wc, d.reshape(1, S5_WIDTH), sc)


def _ssd_kernel(z_ref, xs_ref, b_ref, c_ref, dt_ref, cw_ref, cb_ref, dtb_ref, alog_ref, dvec_ref, nw_ref,
                exp_h_ref, exp_l_ref, o_ref, buf_ref, st_ref):
    q = SSD_CHUNK

    @pl.when(pl.program_id(1) == 0)
    def _():
        buf_ref[0:SUBLANES, :] = jnp.zeros((SUBLANES, SSD_CONV_DIM), F32)
        st_ref[...] = jnp.zeros_like(st_ref)

    buf_ref[SUBLANES:SUBLANES + q, 0:SSD_INNER] = xs_ref[...].astype(F32)
    buf_ref[SUBLANES:SUBLANES + q, SSD_INNER:SSD_INNER + SSD_BC] = b_ref[...]
    buf_ref[SUBLANES:SUBLANES + q, SSD_INNER + SSD_BC:SSD_CONV_DIM] = c_ref[...]
    acc = jnp.broadcast_to(cb_ref[...], (q, SSD_CONV_DIM))
    for k in range(SSD_CONV):
        acc = acc + cw_ref[k:k + 1, :] * buf_ref[pl.ds(SUBLANES - (SSD_CONV - 1) + k, q), :]
    buf_ref[0:SUBLANES, :] = buf_ref[q:q + SUBLANES, :]
    xbc = acc * jax.nn.sigmoid(acc)
    xs = xbc[:, :SSD_INNER]
    bm = xbc[:, SSD_INNER:SSD_INNER + SSD_BC]
    cm = xbc[:, SSD_INNER + SSD_BC:]

    lane = lax.broadcasted_iota(jnp.int32, (q, LANES), 1)
    row = lax.broadcasted_iota(jnp.int32, (q, LANES), 0)
    head_lane = lane < SSD_HEADS
    dt_in = dt_ref[...] + dtb_ref[...]
    dt = jnp.maximum(dt_in, 0.0) + jnp.log1p(jnp.exp(-jnp.abs(dt_in)))
    dt = jnp.where(head_lane, dt, 0.0)
    a = -jnp.exp(alog_ref[...])
    ad = dt * a
    tri = (row >= lane).astype(BF16)
    cs = _exact_dot_rhs(tri, ad)
    cs_t = cs.T
    dt_t = dt.T
    total = cs[q - 1:q, :]
    dec_t = jnp.exp(cs_t[:, q - 1:q] - cs_t)
    w_t = dec_t * dt_t

    exp_h = exp_h_ref[...]
    exp_l = exp_l_ref[...]
    ecs_x = _exact_dot_lhs(jnp.exp(cs), exp_h)
    w_x = _exact_dot_lhs(w_t.T, exp_h)
    cs_cols = _exact_dot_lhs(cs, exp_l)
    tot_x = _exact_dot_lhs(jnp.broadcast_to(jnp.exp(total), (SUBLANES, LANES)), exp_h)[0:1, :]

    bm16 = bm.astype(BF16)
    cm16 = cm.astype(BF16)
    xs16 = xs.astype(BF16)
    low = lane < SSD_STATE
    g_mats = [_dot_nt(jnp.where(low, cm, 0.0).astype(BF16), bm16),
              _dot_nt(jnp.where(low, 0.0, cm).astype(BF16), bm16)]
    causal = row >= lane
    ydiag = []
    for j in range(SSD_HEADS // 2):
        slab = xs[:, j * LANES:(j + 1) * LANES]
        halves = (jnp.where(low, slab, 0.0).astype(BF16), jnp.where(low, 0.0, slab).astype(BF16))
        acc_j = None
        for half in range(2):
            h = 2 * j + half
            g = h // (SSD_HEADS // SSD_GROUPS)
            seg = cs_cols[:, h * LANES:(h + 1) * LANES] - cs_t[h:h + 1, :]
            lmat = jnp.exp(jnp.where(causal, seg, NEG_INF))
            m = (g_mats[g] * lmat * dt_t[h:h + 1, :]).astype(BF16)
            part = _dot(m, halves[half])
            acc_j = part if acc_j is None else acc_j + part
        ydiag.append(acc_j)
    y = jnp.concatenate(ydiag, axis=1)

    st = st_ref[...]
    y = y + _dot(cm16, st.astype(BF16)) * ecs_x
    new = _dot(bm.T.astype(BF16), (xs * w_x).astype(BF16))
    srow = lax.broadcasted_iota(jnp.int32, (LANES, SSD_INNER), 0)
    scol = lax.broadcasted_iota(jnp.int32, (LANES, SSD_INNER), 1)
    same_group = (srow < SSD_STATE) == (scol < SSD_INNER // SSD_GROUPS)
    st_ref[...] = jnp.where(same_group, st * tot_x + new, 0.0)

    y = y + dvec_ref[...] * xs
    zz = z_ref[...].astype(F32)
    gated = y * (zz * jax.nn.sigmoid(zz))
    out = gated * lax.rsqrt(jnp.mean(gated * gated, axis=-1, keepdims=True) + EPS) * nw_ref[...]
    o_ref[...] = out.astype(o_ref.dtype)


def _ssd(big, small, conv_w, conv_b, dt_bias, a_log, dvec, norm_w, *, batch, seq):
    n = batch * seq
    nc = seq // SSD_CHUNK
    q = SSD_CHUNK
    pad = LANES - SSD_HEADS
    dtb = jnp.pad(dt_bias.astype(F32), (0, pad)).reshape(1, LANES)
    alog = jnp.pad(a_log.astype(F32), (0, pad)).reshape(1, LANES)
    d_x = jnp.repeat(dvec.astype(F32), SSD_HEAD_DIM).reshape(1, SSD_INNER)
    heads = np.arange(LANES)[:, None]
    exp_h = jnp.asarray(heads == (np.arange(SSD_INNER)[None, :] // SSD_HEAD_DIM), dtype=BF16)
    exp_l = jnp.asarray(heads == (np.arange(SSD_HEADS * LANES)[None, :] // LANES), dtype=BF16)

    def blk(col, width):
        return pl.BlockSpec((q, width), lambda b, c: (b * nc + c, col))

    def full(shape):
        return pl.BlockSpec(shape, lambda b, c: (0,) * len(shape))

    return pl.pallas_call(
        _ssd_kernel,
        grid=(batch, nc),
        in_specs=[
            blk(0, SSD_INNER), blk(1, SSD_INNER),
            blk(2, LANES), blk(3, LANES), blk(4, LANES),
            full((SSD_CONV, SSD_CONV_DIM)), full((1, SSD_CONV_DIM)),
            full((1, LANES)), full((1, LANES)), full((1, SSD_INNER)), full((1, SSD_INNER)),
            full((LANES, SSD_INNER)), full((LANES, SSD_HEADS * LANES)),
        ],
        out_specs=pl.BlockSpec((q, SSD_INNER), lambda b, c: (b * nc + c, 0)),
        out_shape=jax.ShapeDtypeStruct((n, SSD_INNER), BF16),
        scratch_shapes=[
            pltpu.VMEM((q + SUBLANES, SSD_CONV_DIM), F32),
            pltpu.VMEM((LANES, SSD_INNER), F32),
        ],
        compiler_params=_params("parallel", "arbitrary"),
        name="ssd",
    )(big, big, small, small, small, conv_w.astype(F32), conv_b.astype(F32).reshape(1, SSD_CONV_DIM),
      dtb, alog, d_x, norm_w.astype(F32).reshape(1, SSD_INNER), exp_h, exp_l)


def _merge_kernel(attn_ref, ys5_ref, ssd_ref, ga_ref, gb_ref, gc_ref, x_ref,
                  wa_ref, wga_ref, wgb_ref, wc_ref, wo_ref, o_ref):
    ys5 = ys5_ref[...]
    br_a = _dot(attn_ref[...], wa_ref[...])
    br_b = _dot(ys5, wga_ref[...]) * jax.nn.sigmoid(_dot(ys5, wgb_ref[...]))
    br_c = _dot(ssd_ref[...], wc_ref[...])
    merged = (jax.nn.sigmoid(ga_ref[...].astype(F32)) * br_a + jax.nn.sigmoid(gb_ref[...].astype(F32)) * br_b
              + jax.nn.sigmoid(gc_ref[...].astype(F32)) * br_c)
    o_ref[...] = x_ref[...] + _dot(merged.astype(BF16), wo_ref[...])


def _merge(attn, ys5, ssd, big, x, wa, wga, wgb, wc, wo, *, tm=256):
    n = x.shape[0]

    def rows(width, col=0):
        return pl.BlockSpec((tm, width), lambda i: (i, col))

    def full(a):
        return pl.BlockSpec(a.shape, lambda i: (0, 0))

    return pl.pallas_call(
        _merge_kernel,
        grid=(n // tm,),
        in_specs=[rows(ATTN_WIDTH), rows(S5_WIDTH), rows(SSD_INNER),
                  rows(D_MODEL, 2), rows(D_MODEL, 3), rows(D_MODEL, 4), rows(D_MODEL),
                  full(wa), full(wga), full(wgb), full(wc), full(wo)],
        out_specs=rows(D_MODEL),
        out_shape=jax.ShapeDtypeStruct((n, D_MODEL), F32),
        compiler_params=_params("parallel"),
        name="merge",
    )(attn, ys5, ssd, big, big, big, x, wa, wga, wgb, wc, wo)


def _xattn_kernel(x_ref, g_ref, kv_ref, wq_ref, wo_ref, o_ref):
    x = x_ref[...]
    h = (x * lax.rsqrt(jnp.mean(x * x, axis=-1, keepdims=True) + EPS) * g_ref[...]).astype(BF16)
    q = (_dot(h, wq_ref[...]) * (XA_HEAD_DIM ** -0.5)).astype(BF16)
    kv = kv_ref[...]
    k = kv[:, :XA_WIDTH]
    v = kv[:, XA_WIDTH:]
    lane = lax.broadcasted_iota(jnp.int32, k.shape, 1)
    o = None
    for hd in range(XA_HEADS):
        sel = (lane >= hd * XA_HEAD_DIM) & (lane < (hd + 1) * XA_HEAD_DIM)
        kh = jnp.where(sel, k, 0.0).astype(BF16)
        vh = jnp.where(sel, v, 0.0).astype(BF16)
        s = _dot_nt(q, kh)
        s = s - jnp.max(s, axis=-1, keepdims=True)
        p = jnp.exp(s)
        p = p / jnp.sum(p, axis=-1, keepdims=True)
        part = _dot(p.astype(BF16), vh)
        o = part if o is None else o + part
    o_ref[...] = x + _dot(o.astype(BF16), wo_ref[...])


def _xattn(x, g, kv, wq, wo, *, batch, seq, n_mem, tm=256):
    n = batch * seq
    nt = seq // tm
    return pl.pallas_call(
        _xattn_kernel,
        grid=(batch, nt),
        in_specs=[
            pl.BlockSpec((tm, D_MODEL), lambda b, i: (b * nt + i, 0)),
            pl.BlockSpec((1, D_MODEL), lambda b, i: (0, 0)),
            pl.BlockSpec((n_mem, 2 * XA_WIDTH), lambda b, i: (b, 0)),
            pl.BlockSpec((D_MODEL, XA_WIDTH), lambda b, i: (0, 0)),
            pl.BlockSpec((XA_WIDTH, D_MODEL), lambda b, i: (0, 0)),
        ],
        out_specs=pl.BlockSpec((tm, D_MODEL), lambda b, i: (b * nt + i, 0)),
        out_shape=jax.ShapeDtypeStruct((n, D_MODEL), F32),
        compiler_params=_params("parallel", "parallel"),
        name="xattn",
    )(x, g.reshape(1, D_MODEL), kv, wq, wo)


_PEER_CELLS = [(i, j) for i in range(PEER_TOPK) for j in range(PEER_TOPK) if (i + 1) * (j + 1) <= PEER_TOPK]


def _peer_route_kernel(q_ref, k1_ref, k2_ref, e1_ref, e2_ref, thr_ref, rows_ref, kept_ref):
    tm = q_ref.shape[0]
    keys = (k1_ref[...], k2_ref[...])
    kidx = lax.broadcasted_iota(jnp.int32, (PEER_N_KEYS, tm), 0).astype(F32)

    def store(which, hd, st, rows, vals):
        for r in range(PEER_TOPK):
            rows_ref[which, r, hd:hd + 1, :] = rows[r] - rows[0]
        kept_ref[which, hd] = jnp.where(vals == NEG_INF, st - rows[0], NEG_INF)

    def scores(hd, which):
        c0 = hd * PEER_KEY_DIM + which * PEER_HALF
        return _dot_nt(keys[which], q_ref[:, c0:c0 + PEER_HALF])

    tied = jnp.zeros((1, tm), F32)
    for hd in range(PEER_HEADS):
        for which in range(2):
            st = scores(hd, which)
            vals = st
            rows = []
            for _ in range(PEER_TOPK):
                m = jnp.max(vals, axis=0, keepdims=True)
                vals = jnp.where(vals == m, NEG_INF, vals)
                rows.append(m)
            store(which, hd, st, rows, vals)
            removed = jnp.sum(jnp.where(vals == NEG_INF, 1.0, 0.0), axis=0, keepdims=True)
            tied = jnp.maximum(tied, jnp.abs(removed - float(PEER_TOPK)))

    @pl.when(jnp.max(tied) > 0.0)
    def _():
        for hd in range(PEER_HEADS):
            for which in range(2):
                st = scores(hd, which)
                vals = st
                rows = []
                for _ in range(PEER_TOPK):
                    m = jnp.max(vals, axis=0, keepdims=True)
                    first = jnp.min(jnp.where(vals == m, kidx, float(PEER_N_KEYS)), axis=0, keepdims=True)
                    vals = jnp.where(kidx == first, NEG_INF, vals)
                    rows.append(m)
                store(which, hd, st, rows, vals)

    v1 = [rows_ref[0, r] for r in range(PEER_TOPK)]
    v2 = [rows_ref[1, r] for r in range(PEER_TOPK)]
    cands = [v1[i] + v2[j] for (i, j) in _PEER_CELLS]
    work = cands
    cum = jnp.zeros_like(v1[0])
    tau = jnp.full_like(v1[0], NEG_INF)
    zsum = jnp.zeros_like(v1[0])
    for _ in range(PEER_TOPK):
        m = functools.reduce(jnp.maximum, work)
        cnt = jnp.zeros_like(m)
        nxt = []
        for c in work:
            eq = c == m
            cnt = cnt + jnp.where(eq, 1.0, 0.0)
            nxt.append(jnp.where(eq, NEG_INF, c))
        work = nxt
        open_ = cum < PEER_TOPK
        used = jnp.minimum(cnt, PEER_TOPK - cum)
        zsum = zsum + jnp.where(open_, used * jnp.exp(m), 0.0)
        tau = jnp.where(open_, m, tau)
        cum = cum + cnt
    log_z = jnp.log(zsum)
    e1v = [jnp.exp(v - log_z) for v in v1]
    e2v = [jnp.exp(v) for v in v2]
    thr = jnp.full_like(tau, jnp.inf)
    for (i, j), c in zip(_PEER_CELLS, cands):
        thr = jnp.minimum(thr, jnp.where(c >= tau, e1v[i] * e2v[j], jnp.inf))
    thr_ref[0] = thr
    for hd in range(PEER_HEADS):
        e1_ref[hd, 0] = jnp.exp(kept_ref[0, hd] - log_z[hd:hd + 1, :])
        e2_ref[hd, 0] = jnp.exp(kept_ref[1, hd])


def _peer_route(q, k1, k2):
    n = q.shape[0]
    tm = LANES
    ns = n // tm
    big_shape = jax.ShapeDtypeStruct((PEER_HEADS, ns, PEER_N_KEYS, tm), F32)
    big_spec = pl.BlockSpec((PEER_HEADS, 1, PEER_N_KEYS, tm), lambda i: (0, i, 0, 0))
    return pl.pallas_call(
        _peer_route_kernel,
        grid=(ns,),
        in_specs=[
            pl.BlockSpec((tm, PEER_HEADS * PEER_KEY_DIM), lambda i: (i, 0)),
            pl.BlockSpec((PEER_N_KEYS, PEER_HALF), lambda i: (0, 0)),
            pl.BlockSpec((PEER_N_KEYS, PEER_HALF), lambda i: (0, 0)),
        ],
        out_specs=[big_spec, big_spec, pl.BlockSpec((1, PEER_HEADS, tm), lambda i: (i, 0, 0))],
        out_shape=[big_shape, big_shape, jax.ShapeDtypeStruct((ns, PEER_HEADS, tm), F32)],
        scratch_shapes=[
            pltpu.VMEM((2, PEER_TOPK, PEER_HEADS, tm), F32),
            pltpu.VMEM((2, PEER_HEADS, PEER_N_KEYS, tm), F32),
        ],
        compiler_params=_params("parallel"),
        name="peer_route",
    )(q, k1, k2)


PEER_TE = 1024
PEER_TM = 1024
PEER_STRIP = 256


def _gelu(x):
    c = math.sqrt(2.0 / math.pi)
    inner = x * ((x * x) * (0.044715 * c) + c)
    hx = 0.5 * x
    return hx + hx * jnp.tanh(inner)


def _peer_dense_kernel(x_ref, ht_ref, e1_ref, e2_ref, thr_ref, u_ref, vt_ref, o_ref, acc_ref, act_ref, p_ref):
    j = pl.program_id(1)
    n_slab = PEER_TE // PEER_N_KEYS

    @pl.when(j == 0)
    def _():
        acc_ref[...] = jnp.zeros_like(acc_ref)

    u = u_ref[...]
    vt = vt_ref[...]
    wide = PEER_STRIP
    n_strip = PEER_TM // wide

    def first_matmul(s):
        return _dot(u, ht_ref[s])

    act_ref[0] = first_matmul(0)
    for s in range(n_strip):
        slot = s % 2
        if s + 1 < n_strip:
            act_ref[1 - slot] = first_matmul(s + 1)
        if s >= 1:
            acc_ref[s - 1] += _dot(vt, p_ref[1 - slot])
        for half in range(wide // LANES):
            tc = s * (wide // LANES) + half
            thr = thr_ref[tc]
            for al in range(n_slab):
                a = j * n_slab + al
                w = None
                for hd in range(PEER_HEADS):
                    prod = e1_ref[hd, tc, pl.ds(a, 1), :] * e2_ref[hd, tc]
                    contrib = jnp.where(prod >= thr[hd:hd + 1, :], prod, 0.0)
                    w = contrib if w is None else w + contrib
                rows = slice(al * PEER_N_KEYS, (al + 1) * PEER_N_KEYS)
                lanes = slice(half * LANES, (half + 1) * LANES)
                g = _gelu(act_ref[slot, rows, lanes])
                p_ref[slot, rows, lanes] = (w * g).astype(BF16)
    acc_ref[n_strip - 1] += _dot(vt, p_ref[(n_strip - 1) % 2])

    @pl.when(j == pl.num_programs(1) - 1)
    def _():
        for s in range(n_strip):
            o_ref[s * wide:(s + 1) * wide, :] = x_ref[s * wide:(s + 1) * wide, :] + acc_ref[s].T


def _peer_dense(x, ht, e1, e2, thr, u, vt):
    n = x.shape[0]
    tm, te = PEER_TM, PEER_TE
    ns = tm // LANES
    big_spec = pl.BlockSpec((PEER_HEADS, ns, PEER_N_KEYS, LANES), lambda i, j: (0, i, 0, 0))
    return pl.pallas_call(
        _peer_dense_kernel,
        grid=(n // tm, PEER_N_EXPERTS // te),
        in_specs=[
            pl.BlockSpec((tm, D_MODEL), lambda i, j: (i, 0)),
            pl.BlockSpec((tm // PEER_STRIP, D_MODEL, PEER_STRIP), lambda i, j: (i, 0, 0)),
            big_spec, big_spec,
            pl.BlockSpec((ns, PEER_HEADS, LANES), lambda i, j: (i, 0, 0)),
            pl.BlockSpec((te, D_MODEL), lambda i, j: (j, 0)),
            pl.BlockSpec((D_MODEL, te), lambda i, j: (0, j)),
        ],
        out_specs=pl.BlockSpec((tm, D_MODEL), lambda i, j: (i, 0)),
        out_shape=jax.ShapeDtypeStruct((n, D_MODEL), F32),
        scratch_shapes=[
            pltpu.VMEM((tm // PEER_STRIP, D_MODEL, PEER_STRIP), F32),
            pltpu.VMEM((2, te, PEER_STRIP), F32),
            pltpu.VMEM((2, te, PEER_STRIP), BF16),
        ],
        compiler_params=_params("parallel", "arbitrary"),
        name="peer_dense",
    )(x, ht, e1, e2, thr, u, vt)


def _final_norm_kernel(x_ref, g_ref, o_ref):
    x = x_ref[...]
    o_ref[...] = x * lax.rsqrt(jnp.mean(x * x, axis=-1, keepdims=True) + EPS) * g_ref[...]


def _final_norm(x, g, *, tm=512):
    n, d = x.shape
    return pl.pallas_call(
        _final_norm_kernel,
        grid=(n // tm,),
        in_specs=[pl.BlockSpec((tm, d), lambda i: (i, 0)), pl.BlockSpec((1, d), lambda i: (0, 0))],
        out_specs=pl.BlockSpec((tm, d), lambda i: (i, 0)),
        out_shape=jax.ShapeDtypeStruct((n, d), F32),
        compiler_params=_params("parallel"),
        name="final_norm",
    )(x, g.reshape(1, d))


def _q_head_perm():
    cols = []
    for j in range(ATTN_HEADS // 2):
        for hd in (j, j + ATTN_HEADS // 2):
            cols.extend(range(hd * ATTN_HEAD_DIM, (hd + 1) * ATTN_HEAD_DIM))
    return np.asarray(cols, dtype=np.int32)


def _split_w_in(w):
    o = 0
    q = w[:, o:o + 512]; o += 512
    k = w[:, o:o + 128]; o += 128
    v = w[:, o:o + 128]; o += 128
    u = w[:, o:o + 512]; o += 512
    z = w[:, o:o + 1024]; o += 1024
    xs = w[:, o:o + 1024]; o += 1024
    bm = w[:, o:o + 128]; o += 128
    cm = w[:, o:o + 128]; o += 128
    dt = w[:, o:o + 16]; o += 16
    gates = w[:, o:o + 3072]
    q = q[:, _q_head_perm()]
    big = jnp.concatenate([z, xs, gates, q, u], axis=1).astype(BF16)
    small = jnp.concatenate([k, v, bm, cm, dt, jnp.zeros((w.shape[0], LANES - SSD_HEADS), w.dtype)], axis=1)
    return big, small.astype(BF16)


def kernel(x, mem, norm_mix, w_in, attn_sinks, s5_lambda_re, s5_lambda_im, s5_log_dt, s5_b_re, s5_b_im, s5_c_re, s5_c_im, s5_d, s5_glu_a, s5_glu_b, ssd_conv_w, ssd_conv_b, ssd_dt_bias, ssd_a_log, ssd_d, ssd_norm, w_attn_out, w_ssd_out, w_o, norm_xattn, norm_mem, xa_wq, xa_wkv, xa_wo, norm_ffn, peer_wq, peer_k1, peer_k2, peer_u, peer_v, norm_final):
    batch, seq, d = x.shape
    n = batch * seq
    n_mem = mem.shape[1]
    depth = w_in.shape[0]
    xf = x.reshape(n, d).astype(F32)
    memf = mem.reshape(batch * n_mem, d).astype(F32)
    for i in range(depth):
        w_big, w_small = _split_w_in(w_in[i])
        big = _norm_mm(xf, norm_mix[i], w_big, tm=1024, tn=1024, out_dtype=BF16, name="in_proj_big")
        small = _norm_mm(xf, norm_mix[i], w_small, tm=1024, tn=SMALL_COLS, out_dtype=F32, name="in_proj_small")
        attn = _swa(big, small, attn_sinks[i].astype(F32), batch=batch, seq=seq)
        wb, wc, sc = _s5_prepare(s5_lambda_re[i], s5_lambda_im[i], s5_log_dt[i], s5_b_re[i], s5_b_im[i],
                                 s5_c_re[i], s5_c_im[i])
        ys5 = _s5(big, wb, wc, s5_d[i].astype(F32), sc, batch=batch, seq=seq)
        ssd = _ssd(big, small, ssd_conv_w[i], ssd_conv_b[i], ssd_dt_bias[i], ssd_a_log[i], ssd_d[i], ssd_norm[i],
                   batch=batch, seq=seq)
        xf = _merge(attn, ys5, ssd, big, xf,
                    w_attn_out[i][_q_head_perm(), :].astype(BF16), s5_glu_a[i].astype(BF16),
                    s5_glu_b[i].astype(BF16), w_ssd_out[i].astype(BF16), w_o[i].astype(BF16))
        kv = _norm_mm(memf, norm_mem[i], xa_wkv[i].astype(BF16), tm=batch * n_mem, tn=2 * XA_WIDTH,
                      out_dtype=F32, name="mem_kv")
        xf = _xattn(xf, norm_xattn[i], kv, xa_wq[i].astype(BF16), xa_wo[i].astype(BF16),
                    batch=batch, seq=seq, n_mem=n_mem)
        q, ht = _norm_mm(xf, norm_ffn[i], peer_wq[i].astype(BF16), tm=1024, tn=1024, out_dtype=BF16,
                        with_h=True, name="peer_query")
        e1, e2, thr = _peer_route(q, peer_k1[i].astype(BF16), peer_k2[i].astype(BF16))
        xf = _peer_dense(xf, ht, e1, e2, thr, peer_u[i].astype(BF16), peer_v[i].astype(BF16).T)
    return _final_norm(xf, norm_final.astype(F32)).reshape(batch, seq, d)
```

```python
import functools
import math

import jax
import jax.numpy as jnp
import numpy as np
from jax import lax
from jax.experimental import pallas as pl
from jax.experimental.pallas import tpu as pltpu

F32 = jnp.float32
BF16 = jnp.bfloat16
NEG_INF = float("-inf")

D_MODEL = 1024
EPS = 1e-6
ATTN_HEADS = 8
ATTN_HEAD_DIM = 64
ATTN_WIDTH = ATTN_HEADS * ATTN_HEAD_DIM
ATTN_BLOCK = 128
S5_WIDTH = 512
S5_GROUP = 16
S5_GROUPS = S5_WIDTH // S5_GROUP
S5_STATE = 64
S5_LANES = S5_GROUPS * S5_STATE
S5_CHUNK = 256
S5_SCAN_WIDTH = 512
SSD_INNER = 1024
SSD_HEAD_DIM = 64
SSD_HEADS = SSD_INNER // SSD_HEAD_DIM
SSD_STATE = 64
SSD_GROUPS = 2
SSD_CONV = 4
SSD_CHUNK = 128
SSD_BC = SSD_GROUPS * SSD_STATE
SSD_CONV_DIM = SSD_INNER + 2 * SSD_BC
XA_HEADS = 4
XA_HEAD_DIM = 64
XA_WIDTH = XA_HEADS * XA_HEAD_DIM
PEER_HEADS = 8
PEER_KEY_DIM = 256
PEER_HALF = PEER_KEY_DIM // 2
PEER_N_KEYS = 128
PEER_TOPK = 16
PEER_N_EXPERTS = PEER_N_KEYS * PEER_N_KEYS
LANES = 128
SUBLANES = 8
VMEM_LIMIT_BYTES = 56 * 1024 * 1024
S5_BLOCKS = S5_WIDTH // LANES

BIG_COLS = 6144
SMALL_COLS = 640


def _params(*sem):
    return pltpu.CompilerParams(dimension_semantics=sem, vmem_limit_bytes=VMEM_LIMIT_BYTES)


def _dot(a, b):
    return jnp.dot(a, b, preferred_element_type=F32)


def _dot_nt(a, b):
    return lax.dot_general(a, b, (((1,), (1,)), ((), ())), preferred_element_type=F32)


def _split3(x):
    p1 = x.astype(BF16)
    r1 = x - p1.astype(F32)
    p2 = r1.astype(BF16)
    p3 = (r1 - p2.astype(F32)).astype(BF16)
    return p1, p2, p3


def _exact_dot_rhs(sel, x):
    p1, p2, p3 = _split3(x)
    return _dot(sel, p1) + _dot(sel, p2) + _dot(sel, p3)


def _exact_dot_lhs(x, sel):
    p1, p2, p3 = _split3(x)
    return _dot(p1, sel) + _dot(p2, sel) + _dot(p3, sel)


def _norm_mm_kernel(x_ref, g_ref, w_ref, o_ref, *rest, with_h):
    if with_h:
        hout_ref, h_ref = rest
    else:
        (h_ref,) = rest

    @pl.when(pl.program_id(1) == 0)
    def _():
        x = x_ref[...]
        h = x * lax.rsqrt(jnp.mean(x * x, axis=-1, keepdims=True) + EPS) * g_ref[...]
        h_ref[...] = h.astype(BF16)
        if with_h:
            for s in range(hout_ref.shape[0]):
                hout_ref[s] = h[s * PEER_STRIP:(s + 1) * PEER_STRIP, :].T.astype(BF16)

    o_ref[...] = _dot(h_ref[...], w_ref[...]).astype(o_ref.dtype)


def _norm_mm(x, g, w, *, tm, tn, out_dtype, with_h=False, name):
    n, k = x.shape
    m = w.shape[1]
    assert n % tm == 0 and m % tn == 0
    out_shape = [jax.ShapeDtypeStruct((n, m), out_dtype)]
    out_specs = [pl.BlockSpec((tm, tn), lambda i, j: (i, j))]
    if with_h:
        out_shape.append(jax.ShapeDtypeStruct((n // PEER_STRIP, k, PEER_STRIP), BF16))
        out_specs.append(pl.BlockSpec((tm // PEER_STRIP, k, PEER_STRIP), lambda i, j: (i, 0, 0)))
    res = pl.pallas_call(
        functools.partial(_norm_mm_kernel, with_h=with_h),
        grid=(n // tm, m // tn),
        in_specs=[
            pl.BlockSpec((tm, k), lambda i, j: (i, 0)),
            pl.BlockSpec((1, k), lambda i, j: (0, 0)),
            pl.BlockSpec((k, tn), lambda i, j: (0, j)),
        ],
        out_specs=out_specs,
        out_shape=out_shape,
        scratch_shapes=[pltpu.VMEM((tm, k), BF16)],
        compiler_params=_params("parallel", "arbitrary"),
        name=name,
    )(x, g.reshape(1, k), w)
    return res if with_h else res[0]


def _swa_kernel(sink_ref, q_ref, kc_ref, kp_ref, vc_ref, vp_ref, o_ref):
    nblk = pl.program_id(1)
    blk = ATTN_BLOCK
    q = q_ref[...]
    k = jnp.concatenate([kp_ref[...], kc_ref[...]], axis=0)
    v = jnp.concatenate([vp_ref[...], vc_ref[...]], axis=0).astype(BF16)
    lane_k = lax.broadcasted_iota(jnp.int32, (2 * blk, LANES), 1)
    k_lo = jnp.where(lane_k < ATTN_HEAD_DIM, k, 0.0).astype(BF16)
    k_hi = jnp.where(lane_k < ATTN_HEAD_DIM, 0.0, k).astype(BF16)
    qi = lax.broadcasted_iota(jnp.int32, (blk, 2 * blk), 0)
    kj = lax.broadcasted_iota(jnp.int32, (blk, 2 * blk), 1)
    dist = qi - kj + blk
    valid = (dist >= 0) & (dist < blk) & ((nblk > 0) | (kj >= blk))
    distf = dist.astype(F32)
    lane_o = lax.broadcasted_iota(jnp.int32, (blk, LANES), 1)
    outs = []
    for j in range(ATTN_HEADS // 2):
        qs = (q[:, j * LANES:(j + 1) * LANES] * (ATTN_HEAD_DIM ** -0.5)).astype(BF16)
        res = []
        for half, kk in ((0, k_lo), (1, k_hi)):
            head = j + 4 * half
            slope = 2.0 ** (-8.0 * (head + 1) / ATTN_HEADS)
            s = _dot_nt(qs, kk) - slope * distf
            s = jnp.where(valid, s, NEG_INF)
            sink = sink_ref[head]
            m = jnp.maximum(jnp.max(s, axis=-1, keepdims=True), sink)
            p = jnp.exp(s - m)
            denom = jnp.sum(p, axis=-1, keepdims=True) + jnp.exp(sink - m)
            p = p / denom
            res.append(_dot(p.astype(BF16), v))
        outs.append(jnp.where(lane_o < ATTN_HEAD_DIM, res[0], res[1]))
    o_ref[...] = jnp.concatenate(outs, axis=1).astype(o_ref.dtype)


def _swa(big, small, sinks, *, batch, seq):
    nb = seq // ATTN_BLOCK
    n = batch * seq
    qcol = 5120 // ATTN_WIDTH

    def cur(col):
        return lambda b, i: (b * nb + i, col)

    def prev(col):
        return lambda b, i: (b * nb + jnp.maximum(i - 1, 0), col)

    return pl.pallas_call(
        _swa_kernel,
        grid=(batch, nb),
        in_specs=[
            pl.BlockSpec(memory_space=pltpu.SMEM),
            pl.BlockSpec((ATTN_BLOCK, ATTN_WIDTH), cur(qcol)),
            pl.BlockSpec((ATTN_BLOCK, LANES), cur(0)),
            pl.BlockSpec((ATTN_BLOCK, LANES), prev(0)),
            pl.BlockSpec((ATTN_BLOCK, LANES), cur(1)),
            pl.BlockSpec((ATTN_BLOCK, LANES), prev(1)),
        ],
        out_specs=pl.BlockSpec((ATTN_BLOCK, ATTN_WIDTH), lambda b, i: (b * nb + i, 0)),
        out_shape=jax.ShapeDtypeStruct((n, ATTN_WIDTH), BF16),
        compiler_params=_params("parallel", "arbitrary"),
        name="swa",
    )(sinks, big, small, small, small, small)


def _s5_kernel(u_ref, wb_ref, wc_ref, d_ref, sc_ref, o_ref, xr_ref, xi_ref, cr_ref, ci_ref):
    @pl.when(pl.program_id(1) == 0)
    def _():
        cr_ref[...] = jnp.zeros_like(cr_ref)
        ci_ref[...] = jnp.zeros_like(ci_ref)

    u = u_ref[...].astype(F32)
    ub = u_ref[...]
    blk_states = S5_LANES // S5_BLOCKS
    for jb in range(S5_BLOCKS):
        st = slice(jb * blk_states, (jb + 1) * blk_states)
        bu = _dot(ub[:, jb * LANES:(jb + 1) * LANES], wb_ref[jb])
        xr_ref[:, st] = bu[:, :blk_states]
        xi_ref[:, st] = bu[:, blk_states:]

    def cmul_add(xr, xi, ar, ai, sr, si):
        return xr + (ar * sr - ai * si), xi + (ar * si + ai * sr)

    for w in range(S5_LANES // S5_SCAN_WIDTH):
        sl = slice(w * S5_SCAN_WIDTH, (w + 1) * S5_SCAN_WIDTH)
        consts = [sc_ref[c, :, sl] for c in range(8)]
        a1r, a1i, a2r, a2i, a4r, a4i, pr, pi = consts

        def body(t, carry):
            cr, ci = carry
            r0 = pl.multiple_of(t * SUBLANES, SUBLANES)
            xr = xr_ref[pl.ds(r0, SUBLANES), sl]
            xi = xi_ref[pl.ds(r0, SUBLANES), sl]
            xr, xi = cmul_add(xr, xi, a1r, a1i, pltpu.roll(xr, 1, 0), pltpu.roll(xi, 1, 0))
            xr, xi = cmul_add(xr, xi, a2r, a2i, pltpu.roll(xr, 2, 0), pltpu.roll(xi, 2, 0))
            xr, xi = cmul_add(xr, xi, a4r, a4i, pltpu.roll(xr, 4, 0), pltpu.roll(xi, 4, 0))
            xr, xi = cmul_add(xr, xi, pr, pi, cr, ci)
            xr_ref[pl.ds(r0, SUBLANES), sl] = xr
            xi_ref[pl.ds(r0, SUBLANES), sl] = xi
            return xr[SUBLANES - 1:SUBLANES, :], xi[SUBLANES - 1:SUBLANES, :]

        cr, ci = lax.fori_loop(0, S5_CHUNK // SUBLANES, body, (cr_ref[0:1, sl], ci_ref[0:1, sl]), unroll=4)
        cr_ref[0:1, sl] = cr
        ci_ref[0:1, sl] = ci

    ys = []
    for jb in range(S5_BLOCKS):
        st = slice(jb * blk_states, (jb + 1) * blk_states)
        xcat = jnp.concatenate([xr_ref[:, st].astype(BF16), xi_ref[:, st].astype(BF16)], axis=1)
        ys.append(_dot(xcat, wc_ref[jb]))
    y = jnp.concatenate(ys, axis=1) + d_ref[...] * u
    o_ref[...] = jax.nn.gelu(y).astype(o_ref.dtype)


def _s5_prepare(lam_re, lam_im, log_dt, b_re, b_im, c_re, c_im):
    dt = jnp.exp(log_dt.astype(F32))[:, None]
    lr = lam_re.astype(F32)
    li = lam_im.astype(F32)
    mag = jnp.exp(lr * dt)
    ar = mag * jnp.cos(li * dt)
    ai = mag * jnp.sin(li * dt)
    den = lr * lr + li * li
    nr = ar - 1.0
    wr = (nr * lr + ai * li) / den
    wi = (ai * lr - nr * li) / den
    br_, bi_ = b_re.astype(F32), b_im.astype(F32)
    bbr = wr[..., None] * br_ - wi[..., None] * bi_
    bbi = wr[..., None] * bi_ + wi[..., None] * br_
    eye = jnp.eye(S5_GROUPS, dtype=F32)
    wbr = jnp.einsum('gpi,gh->gihp', bbr, eye).reshape(S5_WIDTH, S5_LANES)
    wbi = jnp.einsum('gpi,gh->gihp', bbi, eye).reshape(S5_WIDTH, S5_LANES)
    bs = S5_LANES // S5_BLOCKS
    wb = jnp.stack([jnp.concatenate([wbr[jb * LANES:(jb + 1) * LANES, jb * bs:(jb + 1) * bs],
                                     wbi[jb * LANES:(jb + 1) * LANES, jb * bs:(jb + 1) * bs]], axis=1)
                    for jb in range(S5_BLOCKS)]).astype(BF16)
    wcr = jnp.einsum('gip,gh->gphi', c_re.astype(F32), eye).reshape(S5_LANES, S5_WIDTH)
    wci = jnp.einsum('gip,gh->gphi', c_im.astype(F32), eye).reshape(S5_LANES, S5_WIDTH)
    wc = jnp.stack([jnp.concatenate([wcr[jb * bs:(jb + 1) * bs, jb * LANES:(jb + 1) * LANES],
                                     -wci[jb * bs:(jb + 1) * bs, jb * LANES:(jb + 1) * LANES]], axis=0)
                    for jb in range(S5_BLOCKS)]).astype(BF16)
    ar_f = ar.reshape(1, S5_LANES)
    ai_f = ai.reshape(1, S5_LANES)

    def cmul(xr, xi, yr, yi):
        return xr * yr - xi * yi, xr * yi + xi * yr

    pows_r, pows_i = [ar_f], [ai_f]
    for _ in range(SUBLANES - 1):
        nr_, ni_ = cmul(pows_r[-1], pows_i[-1], ar_f, ai_f)
        pows_r.append(nr_)
        pows_i.append(ni_)
    t_idx = jnp.arange(SUBLANES)[:, None]

    def masked(k):
        return (jnp.where(t_idx >= k, pows_r[k - 1], 0.0), jnp.where(t_idx >= k, pows_i[k - 1], 0.0))

    a1 = masked(1)
    a2 = masked(2)
    a4 = masked(4)
    pr = jnp.concatenate(pows_r, axis=0)
    pi = jnp.concatenate(pows_i, axis=0)
    sc = jnp.stack([a1[0], a1[1], a2[0], a2[1], a4[0], a4[1], pr, pi], axis=0)
    return wb, wc, sc


def _s5(big, wb, wc, d, sc, *, batch, seq):
    n = batch * seq
    nc = seq // S5_CHUNK
    ucol = 5632 // S5_WIDTH
    return pl.pallas_call(
        _s5_kernel,
        grid=(batch, nc),
        in_specs=[
            pl.BlockSpec((S5_CHUNK, S5_WIDTH), lambda b, c: (b * nc + c, ucol)),
            pl.BlockSpec((S5_BLOCKS, LANES, 2 * S5_LANES // S5_BLOCKS), lambda b, c: (0, 0, 0)),
            pl.BlockSpec((S5_BLOCKS, 2 * S5_LANES // S5_BLOCKS, LANES), lambda b, c: (0, 0, 0)),
            pl.BlockSpec((1, S5_WIDTH), lambda b, c: (0, 0)),
            pl.BlockSpec((8, SUBLANES, S5_LANES), lambda b, c: (0, 0, 0)),
        ],
        out_specs=pl.BlockSpec((S5_CHUNK, S5_WIDTH), lambda b, c: (b * nc + c, 0)),
        out_shape=jax.ShapeDtypeStruct((n, S5_WIDTH), BF16),
        scratch_shapes=[
            pltpu.VMEM((S5_CHUNK, S5_LANES), F32),
            pltpu.VMEM((S5_CHUNK, S5_LANES), F32),
            pltpu.VMEM((SUBLANES, S5_LANES), F32),
            pltpu.VMEM((SUBLANES, S5_LANES), F32),
        ],
        compiler_params=_params("parallel", "arbitrary"),
        name="s5",
    )(big, wb, wc, d.reshape(1, S5_WIDTH), sc)


def _ssd_kernel(z_ref, xs_ref, b_ref, c_ref, dt_ref, cw_ref, cb_ref, dtb_ref, alog_ref, dvec_ref, nw_ref,
                exp_h_ref, exp_l_ref, o_ref, buf_ref, st_ref):
    q = SSD_CHUNK

    @pl.when(pl.program_id(1) == 0)
    def _():
        buf_ref[0:SUBLANES, :] = jnp.zeros((SUBLANES, SSD_CONV_DIM), F32)
        st_ref[...] = jnp.zeros_like(st_ref)

    buf_ref[SUBLANES:SUBLANES + q, 0:SSD_INNER] = xs_ref[...].astype(F32)
    buf_ref[SUBLANES:SUBLANES + q, SSD_INNER:SSD_INNER + SSD_BC] = b_ref[...]
    buf_ref[SUBLANES:SUBLANES + q, SSD_INNER + SSD_BC:SSD_CONV_DIM] = c_ref[...]
    acc = jnp.broadcast_to(cb_ref[...], (q, SSD_CONV_DIM))
    for k in range(SSD_CONV):
        acc = acc + cw_ref[k:k + 1, :] * buf_ref[pl.ds(SUBLANES - (SSD_CONV - 1) + k, q), :]
    buf_ref[0:SUBLANES, :] = buf_ref[q:q + SUBLANES, :]
    xbc = acc * jax.nn.sigmoid(acc)
    xs = xbc[:, :SSD_INNER]
    bm = xbc[:, SSD_INNER:SSD_INNER + SSD_BC]
    cm = xbc[:, SSD_INNER + SSD_BC:]

    lane = lax.broadcasted_iota(jnp.int32, (q, LANES), 1)
    row = lax.broadcasted_iota(jnp.int32, (q, LANES), 0)
    head_lane = lane < SSD_HEADS
    dt_in = dt_ref[...] + dtb_ref[...]
    dt = jnp.maximum(dt_in, 0.0) + jnp.log1p(jnp.exp(-jnp.abs(dt_in)))
    dt = jnp.where(head_lane, dt, 0.0)
    a = -jnp.exp(alog_ref[...])
    ad = dt * a
    tri = (row >= lane).astype(BF16)
    cs = _exact_dot_rhs(tri, ad)
    cs_t = cs.T
    dt_t = dt.T
    total = cs[q - 1:q, :]
    dec_t = jnp.exp(cs_t[:, q - 1:q] - cs_t)
    w_t = dec_t * dt_t

    exp_h = exp_h_ref[...]
    exp_l = exp_l_ref[...]
    ecs_x = _exact_dot_lhs(jnp.exp(cs), exp_h)
    w_x = _exact_dot_lhs(w_t.T, exp_h)
    cs_cols = _exact_dot_lhs(cs, exp_l)
    tot_x = _exact_dot_lhs(jnp.broadcast_to(jnp.exp(total), (SUBLANES, LANES)), exp_h)[0:1, :]

    bm16 = bm.astype(BF16)
    cm16 = cm.astype(BF16)
    xs16 = xs.astype(BF16)
    low = lane < SSD_STATE
    g_mats = [_dot_nt(jnp.where(low, cm, 0.0).astype(BF16), bm16),
              _dot_nt(jnp.where(low, 0.0, cm).astype(BF16), bm16)]
    causal = row >= lane
    ydiag = []
    for j in range(SSD_HEADS // 2):
        slab = xs[:, j * LANES:(j + 1) * LANES]
        halves = (jnp.where(low, slab, 0.0).astype(BF16), jnp.where(low, 0.0, slab).astype(BF16))
        acc_j = None
        for half in range(2):
            h = 2 * j + half
            g = h // (SSD_HEADS // SSD_GROUPS)
            seg = cs_cols[:, h * LANES:(h + 1) * LANES] - cs_t[h:h + 1, :]
            lmat = jnp.exp(jnp.where(causal, seg, NEG_INF))
            m = (g_mats[g] * lmat * dt_t[h:h + 1, :]).astype(BF16)
            part = _dot(m, halves[half])
            acc_j = part if acc_j is None else acc_j + part
        ydiag.append(acc_j)
    y = jnp.concatenate(ydiag, axis=1)

    st = st_ref[...]
    y = y + _dot(cm16, st.astype(BF16)) * ecs_x
    new = _dot(bm.T.astype(BF16), (xs * w_x).astype(BF16))
    srow = lax.broadcasted_iota(jnp.int32, (LANES, SSD_INNER), 0)
    scol = lax.broadcasted_iota(jnp.int32, (LANES, SSD_INNER), 1)
    same_group = (srow < SSD_STATE) == (scol < SSD_INNER // SSD_GROUPS)
    st_ref[...] = jnp.where(same_group, st * tot_x + new, 0.0)

    y = y + dvec_ref[...] * xs
    zz = z_ref[...].astype(F32)
    gated = y * (zz * jax.nn.sigmoid(zz))
    out = gated * lax.rsqrt(jnp.mean(gated * gated, axis=-1, keepdims=True) + EPS) * nw_ref[...]
    o_ref[...] = out.astype(o_ref.dtype)


def _ssd(big, small, conv_w, conv_b, dt_bias, a_log, dvec, norm_w, *, batch, seq):
    n = batch * seq
    nc = seq // SSD_CHUNK
    q = SSD_CHUNK
    pad = LANES - SSD_HEADS
    dtb = jnp.pad(dt_bias.astype(F32), (0, pad)).reshape(1, LANES)
    alog = jnp.pad(a_log.astype(F32), (0, pad)).reshape(1, LANES)
    d_x = jnp.repeat(dvec.astype(F32), SSD_HEAD_DIM).reshape(1, SSD_INNER)
    heads = np.arange(LANES)[:, None]
    exp_h = jnp.asarray(heads == (np.arange(SSD_INNER)[None, :] // SSD_HEAD_DIM), dtype=BF16)
    exp_l = jnp.asarray(heads == (np.arange(SSD_HEADS * LANES)[None, :] // LANES), dtype=BF16)

    def blk(col, width):
        return pl.BlockSpec((q, width), lambda b, c: (b * nc + c, col))

    def full(shape):
        return pl.BlockSpec(shape, lambda b, c: (0,) * len(shape))

    return pl.pallas_call(
        _ssd_kernel,
        grid=(batch, nc),
        in_specs=[
            blk(0, SSD_INNER), blk(1, SSD_INNER),
            blk(2, LANES), blk(3, LANES), blk(4, LANES),
            full((SSD_CONV, SSD_CONV_DIM)), full((1, SSD_CONV_DIM)),
            full((1, LANES)), full((1, LANES)), full((1, SSD_INNER)), full((1, SSD_INNER)),
            full((LANES, SSD_INNER)), full((LANES, SSD_HEADS * LANES)),
        ],
        out_specs=pl.BlockSpec((q, SSD_INNER), lambda b, c: (b * nc + c, 0)),
        out_shape=jax.ShapeDtypeStruct((n, SSD_INNER), BF16),
        scratch_shapes=[
            pltpu.VMEM((q + SUBLANES, SSD_CONV_DIM), F32),
            pltpu.VMEM((LANES, SSD_INNER), F32),
        ],
        compiler_params=_params("parallel", "arbitrary"),
        name="ssd",
    )(big, big, small, small, small, conv_w.astype(F32), conv_b.astype(F32).reshape(1, SSD_CONV_DIM),
      dtb, alog, d_x, norm_w.astype(F32).reshape(1, SSD_INNER), exp_h, exp_l)


def _merge_kernel(attn_ref, ys5_ref, ssd_ref, ga_ref, gb_ref, gc_ref, x_ref,
                  wa_ref, wga_ref, wgb_ref, wc_ref, wo_ref, o_ref):
    ys5 = ys5_ref[...]
    br_a = _dot(attn_ref[...], wa_ref[...])
    br_b = _dot(ys5, wga_ref[...]) * jax.nn.sigmoid(_dot(ys5, wgb_ref[...]))
    br_c = _dot(ssd_ref[...], wc_ref[...])
    merged = (jax.nn.sigmoid(ga_ref[...].astype(F32)) * br_a + jax.nn.sigmoid(gb_ref[...].astype(F32)) * br_b
              + jax.nn.sigmoid(gc_ref[...].astype(F32)) * br_c)
    o_ref[...] = x_ref[...] + _dot(merged.astype(BF16), wo_ref[...])


def _merge(attn, ys5, ssd, big, x, wa, wga, wgb, wc, wo, *, tm=512):
    n = x.shape[0]

    def rows(width, col=0):
        return pl.BlockSpec((tm, width), lambda i: (i, col))

    def full(a):
        return pl.BlockSpec(a.shape, lambda i: (0, 0))

    return pl.pallas_call(
        _merge_kernel,
        grid=(n // tm,),
        in_specs=[rows(ATTN_WIDTH), rows(S5_WIDTH), rows(SSD_INNER),
                  rows(D_MODEL, 2), rows(D_MODEL, 3), rows(D_MODEL, 4), rows(D_MODEL),
                  full(wa), full(wga), full(wgb), full(wc), full(wo)],
        out_specs=rows(D_MODEL),
        out_shape=jax.ShapeDtypeStruct((n, D_MODEL), F32),
        compiler_params=_params("parallel"),
        name="merge",
    )(attn, ys5, ssd, big, big, big, x, wa, wga, wgb, wc, wo)


def _xattn_kernel(x_ref, g_ref, kv_ref, wq_ref, wo_ref, o_ref):
    x = x_ref[...]
    h = (x * lax.rsqrt(jnp.mean(x * x, axis=-1, keepdims=True) + EPS) * g_ref[...]).astype(BF16)
    q = (_dot(h, wq_ref[...]) * (XA_HEAD_DIM ** -0.5)).astype(BF16)
    kv = kv_ref[...]
    k = kv[:, :XA_WIDTH]
    v = kv[:, XA_WIDTH:]
    lane = lax.broadcasted_iota(jnp.int32, k.shape, 1)
    o = None
    for hd in range(XA_HEADS):
        sel = (lane >= hd * XA_HEAD_DIM) & (lane < (hd + 1) * XA_HEAD_DIM)
        kh = jnp.where(sel, k, 0.0).astype(BF16)
        vh = jnp.where(sel, v, 0.0).astype(BF16)
        s = _dot_nt(q, kh)
        s = s - jnp.max(s, axis=-1, keepdims=True)
        p = jnp.exp(s)
        p = p / jnp.sum(p, axis=-1, keepdims=True)
        part = _dot(p.astype(BF16), vh)
        o = part if o is None else o + part
    o_ref[...] = x + _dot(o.astype(BF16), wo_ref[...])


def _xattn(x, g, kv, wq, wo, *, batch, seq, n_mem, tm=512):
    n = batch * seq
    nt = seq // tm
    return pl.pallas_call(
        _xattn_kernel,
        grid=(batch, nt),
        in_specs=[
            pl.BlockSpec((tm, D_MODEL), lambda b, i: (b * nt + i, 0)),
            pl.BlockSpec((1, D_MODEL), lambda b, i: (0, 0)),
            pl.BlockSpec((n_mem, 2 * XA_WIDTH), lambda b, i: (b, 0)),
            pl.BlockSpec((D_MODEL, XA_WIDTH), lambda b, i: (0, 0)),
            pl.BlockSpec((XA_WIDTH, D_MODEL), lambda b, i: (0, 0)),
        ],
        out_specs=pl.BlockSpec((tm, D_MODEL), lambda b, i: (b * nt + i, 0)),
        out_shape=jax.ShapeDtypeStruct((n, D_MODEL), F32),
        compiler_params=_params("parallel", "parallel"),
        name="xattn",
    )(x, g.reshape(1, D_MODEL), kv, wq, wo)


_PEER_CELLS = [(i, j) for i in range(PEER_TOPK) for j in range(PEER_TOPK) if (i + 1) * (j + 1) <= PEER_TOPK]


def _peer_route_kernel(q_ref, k1_ref, k2_ref, e1_ref, e2_ref, thr_ref, rows_ref, kept_ref):
    tm = q_ref.shape[0]
    keys = (k1_ref[...], k2_ref[...])
    kidx = lax.broadcasted_iota(jnp.int32, (PEER_N_KEYS, tm), 0).astype(F32)

    def store(which, hd, st, rows, vals):
        for r in range(PEER_TOPK):
            rows_ref[which, r, hd:hd + 1, :] = rows[r] - rows[0]
        kept_ref[which, hd] = jnp.where(vals == NEG_INF, st - rows[0], NEG_INF)

    def scores(hd, which):
        c0 = hd * PEER_KEY_DIM + which * PEER_HALF
        return _dot_nt(keys[which], q_ref[:, c0:c0 + PEER_HALF])

    tied = jnp.zeros((1, tm), F32)
    for hd in range(PEER_HEADS):
        for which in range(2):
            st = scores(hd, which)
            vals = st
            rows = []
            for _ in range(PEER_TOPK):
                m = jnp.max(vals, axis=0, keepdims=True)
                vals = jnp.where(vals == m, NEG_INF, vals)
                rows.append(m)
            store(which, hd, st, rows, vals)
            removed = jnp.sum(jnp.where(vals == NEG_INF, 1.0, 0.0), axis=0, keepdims=True)
            tied = jnp.maximum(tied, jnp.abs(removed - float(PEER_TOPK)))

    @pl.when(jnp.max(tied) > 0.0)
    def _():
        for hd in range(PEER_HEADS):
            for which in range(2):
                st = scores(hd, which)
                vals = st
                rows = []
                for _ in range(PEER_TOPK):
                    m = jnp.max(vals, axis=0, keepdims=True)
                    first = jnp.min(jnp.where(vals == m, kidx, float(PEER_N_KEYS)), axis=0, keepdims=True)
                    vals = jnp.where(kidx == first, NEG_INF, vals)
                    rows.append(m)
                store(which, hd, st, rows, vals)

    v1 = [rows_ref[0, r] for r in range(PEER_TOPK)]
    v2 = [rows_ref[1, r] for r in range(PEER_TOPK)]
    cands = [v1[i] + v2[j] for (i, j) in _PEER_CELLS]
    work = cands
    cum = jnp.zeros_like(v1[0])
    tau = jnp.full_like(v1[0], NEG_INF)
    zsum = jnp.zeros_like(v1[0])
    for _ in range(PEER_TOPK):
        m = functools.reduce(jnp.maximum, work)
        cnt = jnp.zeros_like(m)
        nxt = []
        for c in work:
            eq = c == m
            cnt = cnt + jnp.where(eq, 1.0, 0.0)
            nxt.append(jnp.where(eq, NEG_INF, c))
        work = nxt
        open_ = cum < PEER_TOPK
        used = jnp.minimum(cnt, PEER_TOPK - cum)
        zsum = zsum + jnp.where(open_, used * jnp.exp(m), 0.0)
        tau = jnp.where(open_, m, tau)
        cum = cum + cnt
    log_z = jnp.log(zsum)
    e1v = [jnp.exp(v - log_z) for v in v1]
    e2v = [jnp.exp(v) for v in v2]
    thr = jnp.full_like(tau, jnp.inf)
    for (i, j), c in zip(_PEER_CELLS, cands):
        thr = jnp.minimum(thr, jnp.where(c >= tau, e1v[i] * e2v[j], jnp.inf))
    thr_ref[0] = thr
    for hd in range(PEER_HEADS):
        e1_ref[hd, 0] = jnp.exp(kept_ref[0, hd] - log_z[hd:hd + 1, :])
        e2_ref[hd, 0] = jnp.exp(kept_ref[1, hd])


def _peer_route(q, k1, k2):
    n = q.shape[0]
    tm = LANES
    ns = n // tm
    big_shape = jax.ShapeDtypeStruct((PEER_HEADS, ns, PEER_N_KEYS, tm), F32)
    big_spec = pl.BlockSpec((PEER_HEADS, 1, PEER_N_KEYS, tm), lambda i: (0, i, 0, 0))
    return pl.pallas_call(
        _peer_route_kernel,
        grid=(ns,),
        in_specs=[
            pl.BlockSpec((tm, PEER_HEADS * PEER_KEY_DIM), lambda i: (i, 0)),
            pl.BlockSpec((PEER_N_KEYS, PEER_HALF), lambda i: (0, 0)),
            pl.BlockSpec((PEER_N_KEYS, PEER_HALF), lambda i: (0, 0)),
        ],
        out_specs=[big_spec, big_spec, pl.BlockSpec((1, PEER_HEADS, tm), lambda i: (i, 0, 0))],
        out_shape=[big_shape, big_shape, jax.ShapeDtypeStruct((ns, PEER_HEADS, tm), F32)],
        scratch_shapes=[
            pltpu.VMEM((2, PEER_TOPK, PEER_HEADS, tm), F32),
            pltpu.VMEM((2, PEER_HEADS, PEER_N_KEYS, tm), F32),
        ],
        compiler_params=_params("parallel"),
        name="peer_route",
    )(q, k1, k2)


PEER_TE = 1024
PEER_TM = 1024
PEER_STRIP = 256


def _gelu(x):
    c = math.sqrt(2.0 / math.pi)
    inner = x * ((x * x) * (0.044715 * c) + c)
    hx = 0.5 * x
    return hx + hx * jnp.tanh(inner)


def _peer_dense_kernel(x_ref, ht_ref, e1_ref, e2_ref, thr_ref, u_ref, vt_ref, o_ref, acc_ref, act_ref, p_ref):
    j = pl.program_id(1)
    n_slab = PEER_TE // PEER_N_KEYS

    @pl.when(j == 0)
    def _():
        acc_ref[...] = jnp.zeros_like(acc_ref)

    u = u_ref[...]
    vt = vt_ref[...]
    wide = PEER_STRIP
    n_strip = PEER_TM // wide

    def first_matmul(s):
        return _dot(u, ht_ref[s])

    act_ref[0] = first_matmul(0)
    for s in range(n_strip):
        slot = s % 2
        if s + 1 < n_strip:
            act_ref[1 - slot] = first_matmul(s + 1)
        if s >= 1:
            acc_ref[s - 1] += _dot(vt, p_ref[1 - slot])
        for half in range(wide // LANES):
            tc = s * (wide // LANES) + half
            thr = thr_ref[tc]
            for al in range(n_slab):
                a = j * n_slab + al
                w = None
                for hd in range(PEER_HEADS):
                    prod = e1_ref[hd, tc, pl.ds(a, 1), :] * e2_ref[hd, tc]
                    contrib = jnp.where(prod >= thr[hd:hd + 1, :], prod, 0.0)
                    w = contrib if w is None else w + contrib
                rows = slice(al * PEER_N_KEYS, (al + 1) * PEER_N_KEYS)
                lanes = slice(half * LANES, (half + 1) * LANES)
                g = _gelu(act_ref[slot, rows, lanes])
                p_ref[slot, rows, lanes] = (w * g).astype(BF16)
    acc_ref[n_strip - 1] += _dot(vt, p_ref[(n_strip - 1) % 2])

    @pl.when(j == pl.num_programs(1) - 1)
    def _():
        for s in range(n_strip):
            o_ref[s * wide:(s + 1) * wide, :] = x_ref[s * wide:(s + 1) * wide, :] + acc_ref[s].T


def _peer_dense(x, ht, e1, e2, thr, u, vt):
    n = x.shape[0]
    tm, te = PEER_TM, PEER_TE
    ns = tm // LANES
    big_spec = pl.BlockSpec((PEER_HEADS, ns, PEER_N_KEYS, LANES), lambda i, j: (0, i, 0, 0))
    return pl.pallas_call(
        _peer_dense_kernel,
        grid=(n // tm, PEER_N_EXPERTS // te),
        in_specs=[
            pl.BlockSpec((tm, D_MODEL), lambda i, j: (i, 0)),
            pl.BlockSpec((tm // PEER_STRIP, D_MODEL, PEER_STRIP), lambda i, j: (i, 0, 0)),
            big_spec, big_spec,
            pl.BlockSpec((ns, PEER_HEADS, LANES), lambda i, j: (i, 0, 0)),
            pl.BlockSpec((te, D_MODEL), lambda i, j: (j, 0)),
            pl.BlockSpec((D_MODEL, te), lambda i, j: (0, j)),
        ],
        out_specs=pl.BlockSpec((tm, D_MODEL), lambda i, j: (i, 0)),
        out_shape=jax.ShapeDtypeStruct((n, D_MODEL), F32),
        scratch_shapes=[
            pltpu.VMEM((tm // PEER_STRIP, D_MODEL, PEER_STRIP), F32),
            pltpu.VMEM((2, te, PEER_STRIP), F32),
            pltpu.VMEM((2, te, PEER_STRIP), BF16),
        ],
        compiler_params=_params("parallel", "arbitrary"),
        name="peer_dense",
    )(x, ht, e1, e2, thr, u, vt)


def _final_norm_kernel(x_ref, g_ref, o_ref):
    x = x_ref[...]
    o_ref[...] = x * lax.rsqrt(jnp.mean(x * x, axis=-1, keepdims=True) + EPS) * g_ref[...]


def _final_norm(x, g, *, tm=512):
    n, d = x.shape
    return pl.pallas_call(
        _final_norm_kernel,
        grid=(n // tm,),
        in_specs=[pl.BlockSpec((tm, d), lambda i: (i, 0)), pl.BlockSpec((1, d), lambda i: (0, 0))],
        out_specs=pl.BlockSpec((tm, d), lambda i: (i, 0)),
        out_shape=jax.ShapeDtypeStruct((n, d), F32),
        compiler_params=_params("parallel"),
        name="final_norm",
    )(x, g.reshape(1, d))


def _q_head_perm():
    cols = []
    for j in range(ATTN_HEADS // 2):
        for hd in (j, j + ATTN_HEADS // 2):
            cols.extend(range(hd * ATTN_HEAD_DIM, (hd + 1) * ATTN_HEAD_DIM))
    return np.asarray(cols, dtype=np.int32)


def _split_w_in(w):
    o = 0
    q = w[:, o:o + 512]; o += 512
    k = w[:, o:o + 128]; o += 128
    v = w[:, o:o + 128]; o += 128
    u = w[:, o:o + 512]; o += 512
    z = w[:, o:o + 1024]; o += 1024
    xs = w[:, o:o + 1024]; o += 1024
    bm = w[:, o:o + 128]; o += 128
    cm = w[:, o:o + 128]; o += 128
    dt = w[:, o:o + 16]; o += 16
    gates = w[:, o:o + 3072]
    q = q[:, _q_head_perm()]
    big = jnp.concatenate([z, xs, gates, q, u], axis=1).astype(BF16)
    small = jnp.concatenate([k, v, bm, cm, dt, jnp.zeros((w.shape[0], LANES - SSD_HEADS), w.dtype)], axis=1)
    return big, small.astype(BF16)


def kernel(x, mem, norm_mix, w_in, attn_sinks, s5_lambda_re, s5_lambda_im, s5_log_dt, s5_b_re, s5_b_im, s5_c_re, s5_c_im, s5_d, s5_glu_a, s5_glu_b, ssd_conv_w, ssd_conv_b, ssd_dt_bias, ssd_a_log, ssd_d, ssd_norm, w_attn_out, w_ssd_out, w_o, norm_xattn, norm_mem, xa_wq, xa_wkv, xa_wo, norm_ffn, peer_wq, peer_k1, peer_k2, peer_u, peer_v, norm_final):
    batch, seq, d = x.shape
    n = batch * seq
    n_mem = mem.shape[1]
    depth = w_in.shape[0]
    xf = x.reshape(n, d).astype(F32)
    memf = mem.reshape(batch * n_mem, d).astype(F32)
    for i in range(depth):
        w_big, w_small = _split_w_in(w_in[i])
        big = _norm_mm(xf, norm_mix[i], w_big, tm=1024, tn=1024, out_dtype=BF16, name="in_proj_big")
        small = _norm_mm(xf, norm_mix[i], w_small, tm=1024, tn=SMALL_COLS, out_dtype=F32, name="in_proj_small")
        attn = _swa(big, small, attn_sinks[i].astype(F32), batch=batch, seq=seq)
        wb, wc, sc = _s5_prepare(s5_lambda_re[i], s5_lambda_im[i], s5_log_dt[i], s5_b_re[i], s5_b_im[i],
                                 s5_c_re[i], s5_c_im[i])
        ys5 = _s5(big, wb, wc, s5_d[i].astype(F32), sc, batch=batch, seq=seq)
        ssd = _ssd(big, small, ssd_conv_w[i], ssd_conv_b[i], ssd_dt_bias[i], ssd_a_log[i], ssd_d[i], ssd_norm[i],
                   batch=batch, seq=seq)
        xf = _merge(attn, ys5, ssd, big, xf,
                    w_attn_out[i][_q_head_perm(), :].astype(BF16), s5_glu_a[i].astype(BF16),
                    s5_glu_b[i].astype(BF16), w_ssd_out[i].astype(BF16), w_o[i].astype(BF16))
        kv = _norm_mm(memf, norm_mem[i], xa_wkv[i].astype(BF16), tm=batch * n_mem, tn=2 * XA_WIDTH,
                      out_dtype=F32, name="mem_kv")
        xf = _xattn(xf, norm_xattn[i], kv, xa_wq[i].astype(BF16), xa_wo[i].astype(BF16),
                    batch=batch, seq=seq, n_mem=n_mem)
        q, ht = _norm_mm(xf, norm_ffn[i], peer_wq[i].astype(BF16), tm=1024, tn=1024, out_dtype=BF16,
                        with_h=True, name="peer_query")
        e1, e2, thr = _peer_route(q, peer_k1[i].astype(BF16), peer_k2[i].astype(BF16))
        xf = _peer_dense(xf, ht, e1, e2, thr, peer_u[i].astype(BF16), peer_v[i].astype(BF16).T)
    return _final_norm(xf, norm_final.astype(F32)).reshape(batch, seq, d)
```
